```python
import math
import jax, jax.numpy as jnp
from jax import lax
import numpy as np

D_MODEL = 2048
BATCH = 16
SEQ = 2048
DEPTH = 1
DEC_BATCH = 128
DEC_SEQ = 1
PAST_LEN = 16384
PAGE_SIZE = 128

N_META = 16
HEAD_DIM = 64
ATTN_WIDTH = D_MODEL // 2
N_HEADS = ATTN_WIDTH // HEAD_DIM
N_KV_HEADS = N_HEADS // 4
GQA_GROUP = N_HEADS // N_KV_HEADS
WINDOW = 128
BLOCK = 128
ROT_DIM = HEAD_DIM // 4
ROPE_THETA = 500000.0
POOL_WIDTH = D_MODEL - ATTN_WIDTH
POOL_WINDOWS = (2, 4, 8, 16)
N_POOL_GROUPS = len(POOL_WINDOWS)
POOL_GROUP_WIDTH = POOL_WIDTH // N_POOL_GROUPS
POOL_HIST = max(POOL_WINDOWS) - 1
Q_COLS = N_HEADS * HEAD_DIM
KV_COLS = N_KV_HEADS * HEAD_DIM
IN_COLS = Q_COLS + 2 * KV_COLS + POOL_WIDTH
D_FF = ((-(-8 * D_MODEL // 3)) + 255) // 256 * 256
DEEPNORM_ALPHA = (2.0 * DEPTH) ** 0.25
DEEPNORM_BETA = (8.0 * DEPTH) ** -0.25
LN_EPS = 1e-5
NEG_INF = -1e30

kernel_name = "hymba_swa_sink_multiscale_pool_deepnorm_step"


def layer_norm(x, g, b):
    xf = x.astype(jnp.float32)
    mu = jnp.mean(xf, -1, keepdims=True)
    var = jnp.mean(jnp.square(xf - mu), -1, keepdims=True)
    return ((xf - mu) * lax.rsqrt(var + LN_EPS) * g.astype(jnp.float32) + b.astype(jnp.float32)).astype(x.dtype)


def partial_rope(x, pos):
    half = ROT_DIM // 2
    inv_freq = ROPE_THETA ** (-jnp.arange(half, dtype=jnp.float32) * 2.0 / ROT_DIM)
    ang = pos.astype(jnp.float32)[:, None] * inv_freq
    shape = (pos.shape[0],) + (1,) * (x.ndim - 3) + (half,)
    cos = jnp.cos(ang).reshape(shape)
    sin = jnp.sin(ang).reshape(shape)
    x1 = x[..., :half].astype(jnp.float32)
    x2 = x[..., half:ROT_DIM].astype(jnp.float32)
    rot = jnp.concatenate([x1 * cos - x2 * sin, x2 * cos + x1 * sin], -1).astype(x.dtype)
    return jnp.concatenate([rot, x[..., ROT_DIM:]], -1)


def mixer_inputs(h, pos, w_in, b_in):
    bn, t, _ = h.shape
    z = jnp.einsum('btd,de->bte', h, w_in) + b_in
    q = z[..., :Q_COLS].reshape(bn, t, N_KV_HEADS, GQA_GROUP, HEAD_DIM)
    k = z[..., Q_COLS:Q_COLS + KV_COLS].reshape(bn, t, N_KV_HEADS, HEAD_DIM)
    v = z[..., Q_COLS + KV_COLS:Q_COLS + 2 * KV_COLS].reshape(bn, t, N_KV_HEADS, HEAD_DIM)
    u = z[..., Q_COLS + 2 * KV_COLS:]
    return partial_rope(q, pos), partial_rope(k, pos), v, u


def sink_attention(q, k, v, mask, sinks):
    s = jnp.einsum('...qhgd,...shd->...hgqs', q, k, preferred_element_type=jnp.float32) * (1.0 / math.sqrt(HEAD_DIM))
    s = jnp.where(mask, s, NEG_INF)
    sink = sinks.astype(jnp.float32)[:, :, None, None]
    m = jnp.maximum(jnp.max(s, -1, keepdims=True), sink)
    p = jnp.exp(s - m)
    denom = jnp.sum(p, -1, keepdims=True) + jnp.exp(sink - m)
    return jnp.einsum('...hgqs,...shd->...qhgd', (p / denom).astype(v.dtype), v)


def prompt_window_attention(q, k, v, sinks):
    bn, L = q.shape[:2]
    pad = (-L) % BLOCK
    lp = L + pad
    nb = lp // BLOCK
    padt = lambda a: jnp.pad(a, ((0, 0), (pad, 0)) + ((0, 0),) * (a.ndim - 2))
    qb = padt(q).reshape(bn, nb, BLOCK, N_KV_HEADS, GQA_GROUP, HEAD_DIM)
    kb = padt(k).reshape(bn, nb, BLOCK, N_KV_HEADS, HEAD_DIM)
    vb = padt(v).reshape(bn, nb, BLOCK, N_KV_HEADS, HEAD_DIM)
    shift = lambda a: jnp.concatenate([jnp.zeros_like(a[:, :1]), a[:, :-1]], 1)
    kband = jnp.concatenate([shift(kb), kb], 2)
    vband = jnp.concatenate([shift(vb), vb], 2)
    qpos = (jnp.arange(lp, dtype=jnp.int32) - pad).reshape(nb, BLOCK)
    kpos = jnp.concatenate([qpos - BLOCK, qpos], 1)
    diff = qpos[:, :, None] - kpos[:, None, :]
    mask = (diff >= 0) & (diff <= WINDOW) & (kpos[:, None, :] >= 0)
    o = sink_attention(qb, kband, vband, mask[:, None, None], sinks)
    return o.reshape(bn, lp, Q_COLS)[:, pad:]


def sample_window_attention(q, k_all, v_all, pos_new, sinks):
    bn, t = q.shape[:2]
    s = k_all.shape[1]
    kpos = pos_new[0] - (s - t) + jnp.arange(s, dtype=jnp.int32)
    diff = pos_new[:, None] - kpos[None, :]
    mask = (diff >= 0) & (diff <= WINDOW)
    return sink_attention(q, k_all, v_all, mask, sinks).reshape(bn, t, Q_COLS)


def pool_mixer(u_hist, u_new, pos_new, w_pool, pool_scale):
    bn, t, _ = u_new.shape
    ext = jnp.concatenate([u_hist, u_new], 1)
    extf = ext.astype(jnp.float32)
    csum = jnp.cumsum(jnp.pad(extf, ((0, 0), (1, 0), (0, 0))), axis=1)
    hi = csum[:, POOL_HIST + 1:]
    cur = extf[:, POOL_HIST:]
    outs = []
    for gi, w in enumerate(POOL_WINDOWS):
        sl = slice(gi * POOL_GROUP_WIDTH, (gi + 1) * POOL_GROUP_WIDTH)
        lo = csum[:, POOL_HIST + 1 - w:POOL_HIST + 1 - w + t, sl]
        cnt = jnp.minimum(w, pos_new + 1).astype(jnp.float32)[:, None]
        outs.append((hi[..., sl] - lo) / cnt - cur[..., sl])
    d = jnp.stack(outs, 2).astype(u_new.dtype)
    y = jnp.einsum('btgc,gcd->btgd', d, w_pool).reshape(bn, t, POOL_WIDTH) * pool_scale
    return y, ext[:, -POOL_HIST:]


def post_sublayers(h, attn_out, pool_out, w_o, ln1_g, ln1_b, w_gate, w_up, w_down, ln2_g, ln2_b):
    mix = jnp.einsum('bte,ed->btd', jnp.concatenate([attn_out, pool_out], -1), w_o)
    h1 = layer_norm(DEEPNORM_ALPHA * h + mix, ln1_g, ln1_b)
    a = jax.nn.silu(jnp.einsum('btd,df->btf', h1, w_gate)) * jnp.einsum('btd,df->btf', h1, w_up)
    ff = jnp.einsum('btf,fd->btd', a, w_down)
    return layer_norm(DEEPNORM_ALPHA * h1 + ff, ln2_g, ln2_b)


def setup_inputs(seed: int = 0) -> dict:
    key = jax.random.key(seed)
    ks = jax.random.split(key, 24)
    f32 = jnp.float32
    nrm = lambda k, shape, s: jax.random.normal(k, shape, f32) * s
    w_keep = min(WINDOW, PAST_LEN)
    return {
        "x_prompt": nrm(ks[0], (BATCH, SEQ, D_MODEL), 1.0),
        "x_sample": nrm(ks[1], (DEC_BATCH, DEC_SEQ, D_MODEL), 1.0),
        "cache_k": nrm(ks[2], (DEPTH, DEC_BATCH, w_keep, N_KV_HEADS, HEAD_DIM), 1.0),
        "cache_v": nrm(ks[3], (DEPTH, DEC_BATCH, w_keep, N_KV_HEADS, HEAD_DIM), 1.0),
        "state_pool": nrm(ks[4], (DEPTH, DEC_BATCH, POOL_HIST, POOL_WIDTH), 1.0),
        "meta_tokens": nrm(ks[5], (N_META, D_MODEL), 1.0),
        "ln_in_g": 1.0 + nrm(ks[6], (D_MODEL,), 0.02),
        "ln_in_b": nrm(ks[7], (D_MODEL,), 0.02),
        "w_in": nrm(ks[8], (DEPTH, D_MODEL, IN_COLS), D_MODEL ** -0.5),
        "b_in": nrm(ks[9], (DEPTH, IN_COLS), 0.02),
        "attn_sinks": nrm(ks[10], (DEPTH, N_HEADS), 1.0),
        "w_pool": nrm(ks[11], (DEPTH, N_POOL_GROUPS, POOL_GROUP_WIDTH, POOL_GROUP_WIDTH), POOL_GROUP_WIDTH ** -0.5),
        "pool_scale": 1.0 + nrm(ks[12], (DEPTH, POOL_WIDTH), 0.1),
        "w_o": nrm(ks[13], (DEPTH, D_MODEL, D_MODEL), D_MODEL ** -0.5 * DEEPNORM_BETA),
        "ln1_g": 1.0 + nrm(ks[14], (DEPTH, D_MODEL), 0.02),
        "ln1_b": nrm(ks[15], (DEPTH, D_MODEL), 0.02),
        "w_gate": nrm(ks[16], (DEPTH, D_MODEL, D_FF), D_MODEL ** -0.5),
        "w_up": nrm(ks[17], (DEPTH, D_MODEL, D_FF), D_MODEL ** -0.5),
        "w_down": nrm(ks[18], (DEPTH, D_FF, D_MODEL), D_FF ** -0.5 * DEEPNORM_BETA),
        "ln2_g": 1.0 + nrm(ks[19], (DEPTH, D_MODEL), 0.02),
        "ln2_b": nrm(ks[20], (DEPTH, D_MODEL), 0.02),
    }


def reference(x_prompt, x_sample, cache_k, cache_v, state_pool, meta_tokens, ln_in_g, ln_in_b,
              w_in, b_in, attn_sinks, w_pool, pool_scale, w_o, ln1_g, ln1_b,
              w_gate, w_up, w_down, ln2_g, ln2_b):
    bp = x_prompt.shape[0]
    t_s = x_sample.shape[1]
    w_keep = cache_k.shape[2]
    meta = jnp.broadcast_to(meta_tokens.astype(x_prompt.dtype)[None], (bp, N_META, D_MODEL))
    hp = layer_norm(jnp.concatenate([meta, x_prompt], 1), ln_in_g, ln_in_b)
    hs = layer_norm(x_sample, ln_in_g, ln_in_b)
    pos_p = jnp.arange(hp.shape[1], dtype=jnp.int32)
    pos_s = PAST_LEN + jnp.arange(t_s, dtype=jnp.int32)
    nkp, nvp, nup, nks, nvs, nus = [], [], [], [], [], []
    for l in range(DEPTH):
        sinks = attn_sinks[l].reshape(N_KV_HEADS, GQA_GROUP)
        ffn_args = (w_o[l], ln1_g[l], ln1_b[l], w_gate[l], w_up[l], w_down[l], ln2_g[l], ln2_b[l])
        qp, kp, vp, up = mixer_inputs(hp, pos_p, w_in[l], b_in[l])
        ap = prompt_window_attention(qp, kp, vp, sinks)
        pp, hist_p = pool_mixer(jnp.zeros((bp, POOL_HIST, POOL_WIDTH), up.dtype), up, pos_p, w_pool[l], pool_scale[l])
        nkp.append(kp[:, -WINDOW:]); nvp.append(vp[:, -WINDOW:]); nup.append(hist_p)
        hp = post_sublayers(hp, ap, pp, *ffn_args)
        qs, ks_, vs, us = mixer_inputs(hs, pos_s, w_in[l], b_in[l])
        k_all = jnp.concatenate([cache_k[l], ks_], 1)
        v_all = jnp.concatenate([cache_v[l], vs], 1)
        a_s = sample_window_attention(qs, k_all, v_all, pos_s, sinks)
        ps, hist_s = pool_mixer(state_pool[l], us, pos_s, w_pool[l], pool_scale[l])
        nks.append(k_all[:, -w_keep:]); nvs.append(v_all[:, -w_keep:]); nus.append(hist_s)
        hs = post_sublayers(hs, a_s, ps, *ffn_args)
    y_prompt = hp[:, N_META:]
    y_sample = hs
    return (y_prompt, y_sample, jnp.stack(nkp), jnp.stack(nvp), jnp.stack(nup), jnp.stack(nks), jnp.stack(nvs), jnp.stack(nus))
```

```python
import functools
import math

import jax
import jax.numpy as jnp
from jax import lax
from jax.experimental import pallas as pl
from jax.experimental.pallas import tpu as pltpu

D_MODEL = 2048
N_META = 16
HEAD_DIM = 64
N_KV = 4
GQA = 4
Q_COLS = N_KV * GQA * HEAD_DIM
KV_COLS = N_KV * HEAD_DIM
POOL_W = 1024
POOL_WINDOWS = (2, 4, 8, 16)
POOL_GW = POOL_W // len(POOL_WINDOWS)
POOL_HIST = 15
IN_COLS = Q_COLS + 2 * KV_COLS + POOL_W
D_FF = 5632
WINDOW = 128
BLOCK = 128
ROT_DIM = 16
ROPE_THETA = 500000.0
PAST_LEN = 16384
ALPHA = 2.0 ** 0.25
LN_EPS = 1e-5
NEG_INF = -1e30
QK_SCALE = 1.0 / math.sqrt(HEAD_DIM)

V7X_VMEM_LIMIT = 56 * 1024 * 1024

BF16 = jnp.bfloat16
F32 = jnp.float32


def _layer_norm(x, g, b):
    mu = jnp.mean(x, axis=-1, keepdims=True)
    xc = x - mu
    var = jnp.mean(xc * xc, axis=-1, keepdims=True)
    return xc * lax.rsqrt(var + LN_EPS) * g + b


def _params(sem):
    return pltpu.CompilerParams(dimension_semantics=sem, vmem_limit_bytes=V7X_VMEM_LIMIT)


def _inproj_kernel(x_ref, g_ref, b_ref, w_ref, bias_ref, c_ref, s1_ref, s2_ref,
                   h_ref, q_ref, k_ref, v_ref, u_ref):
    h = _layer_norm(x_ref[...], g_ref[...], b_ref[...])
    h_ref[...] = h
    z = jnp.dot(h.astype(BF16), w_ref[...], preferred_element_type=F32) + bias_ref[...]
    cos = c_ref[...]
    s_lo = s1_ref[...]
    s_hi = s2_ref[...]

    def rope(zc):
        return zc * cos + pltpu.roll(zc, 128 - ROT_DIM // 2, 1) * s_lo + pltpu.roll(zc, ROT_DIM // 2, 1) * s_hi

    for c in range(Q_COLS // 128):
        zc = z[:, c * 128:(c + 1) * 128]
        q_ref[:, c * 128:(c + 1) * 128] = (rope(zc) * QK_SCALE).astype(BF16)
    for c in range(KV_COLS // 128):
        zc = z[:, Q_COLS + c * 128:Q_COLS + (c + 1) * 128]
        k_ref[:, c * 128:(c + 1) * 128] = rope(zc)
    v_ref[...] = z[:, Q_COLS + KV_COLS:Q_COLS + 2 * KV_COLS]
    u_ref[...] = z[:, Q_COLS + 2 * KV_COLS:]


def _inproj(x, ln_g, ln_b, w_in, b_in, tabs, tm):
    rows = x.shape[0]
    t_rows = tabs[0].shape[0]
    n_t = t_rows // tm
    row = lambda i: (i, 0)
    fixed = lambda i: (0, 0)
    tab = lambda i: (i % n_t, 0)
    return pl.pallas_call(
        _inproj_kernel,
        grid=(rows // tm,),
        in_specs=[
            pl.BlockSpec((tm, D_MODEL), row),
            pl.BlockSpec((1, D_MODEL), fixed),
            pl.BlockSpec((1, D_MODEL), fixed),
            pl.BlockSpec((D_MODEL, IN_COLS), fixed),
            pl.BlockSpec((1, IN_COLS), fixed),
            pl.BlockSpec((tm, 128), tab),
            pl.BlockSpec((tm, 128), tab),
            pl.BlockSpec((tm, 128), tab),
        ],
        out_specs=[
            pl.BlockSpec((tm, D_MODEL), row),
            pl.BlockSpec((tm, Q_COLS), row),
            pl.BlockSpec((tm, KV_COLS), row),
            pl.BlockSpec((tm, KV_COLS), row),
            pl.BlockSpec((tm, POOL_W), row),
        ],
        out_shape=[
            jax.ShapeDtypeStruct((rows, D_MODEL), F32),
            jax.ShapeDtypeStruct((rows, Q_COLS), BF16),
            jax.ShapeDtypeStruct((rows, KV_COLS), F32),
            jax.ShapeDtypeStruct((rows, KV_COLS), F32),
            jax.ShapeDtypeStruct((rows, POOL_W), F32),
        ],
        compiler_params=_params(("parallel",)),
        name="inproj",
    )(x, ln_g, ln_b, w_in, b_in, *tabs)


def _outproj_ln(mix_bf16, h, wo_ref, g_ref, b_ref):
    mix = jnp.dot(mix_bf16, wo_ref[...], preferred_element_type=F32)
    return _layer_norm(ALPHA * h + mix, g_ref[...], b_ref[...])


def _pool_group_matmul(d, g, wpool_ref, pscale_ref):
    cs = slice(g * POOL_GW, (g + 1) * POOL_GW)
    y = jnp.dot(d.astype(BF16), wpool_ref[g], preferred_element_type=F32)
    return (y * pscale_ref[:, cs]).astype(BF16)


def _prompt_mixer_kernel(sinks_ref, q_ref, k_ref, khalo_ref, kmeta_ref, v_ref, vhalo_ref, vmeta_ref,
                         u_ref, uhalo_ref, umeta_ref, h_ref, wpool_ref, pscale_ref, wo_ref, g_ref, b_ref,
                         h1_ref, kx_ref, vx_ref, ux_ref, mix_ref, *, tm):
    j = pl.program_id(1)
    n_blk = tm // BLOCK
    lane_head = lax.broadcasted_iota(jnp.int32, (1, KV_COLS), 1) // HEAD_DIM

    def put_kv(dst_ref, rows, val):
        for hh in range(N_KV):
            dst_ref[hh, rows, :] = jnp.where(lane_head == hh, val, 0.0).astype(BF16)

    pad = BLOCK - N_META

    @pl.when(j == 0)
    def _():
        zeros = jnp.zeros((pad, KV_COLS), F32)
        put_kv(kx_ref, slice(0, pad), zeros)
        put_kv(vx_ref, slice(0, pad), zeros)
        put_kv(kx_ref, slice(pad, BLOCK), kmeta_ref[...])
        put_kv(vx_ref, slice(pad, BLOCK), vmeta_ref[...])
        ux_ref[0:N_META, :] = umeta_ref[...]

    @pl.when(j > 0)
    def _():
        put_kv(kx_ref, slice(0, BLOCK), khalo_ref[...])
        put_kv(vx_ref, slice(0, BLOCK), vhalo_ref[...])
        ux_ref[0:N_META, :] = uhalo_ref[...]

    put_kv(kx_ref, slice(BLOCK, BLOCK + tm), k_ref[...])
    put_kv(vx_ref, slice(BLOCK, BLOCK + tm), v_ref[...])
    ux_ref[N_META:, :] = u_ref[...]

    for g, w in enumerate(POOL_WINDOWS):
        cs = slice(g * POOL_GW, (g + 1) * POOL_GW)
        cur = ux_ref[N_META:N_META + tm, cs]
        acc = cur
        for s in range(1, w):
            acc = acc + ux_ref[N_META - s:N_META - s + tm, cs]
        d = acc * (1.0 / w) - cur
        mix_ref[:, Q_COLS + g * POOL_GW:Q_COLS + (g + 1) * POOL_GW] = _pool_group_matmul(d, g, wpool_ref, pscale_ref)

    n_rows = GQA * BLOCK
    qi = lax.broadcasted_iota(jnp.int32, (n_rows, 2 * BLOCK), 0) % BLOCK
    ki = lax.broadcasted_iota(jnp.int32, (n_rows, 2 * BLOCK), 1)
    band = (ki >= qi) & (ki <= qi + WINDOW)
    row_group = lax.broadcasted_iota(jnp.int32, (n_rows, 1), 0) // BLOCK

    def block_body(blk, carry):
        r0 = pl.multiple_of(blk * BLOCK, BLOCK)
        first_key = jnp.where((j == 0) & (blk == 0), pad, 0)
        mask = band & (ki >= first_key)
        qs = jnp.concatenate(
            [q_ref[pl.ds(r0, BLOCK), gg * KV_COLS:(gg + 1) * KV_COLS] for gg in range(GQA)], axis=0)
        o = jnp.zeros((n_rows, KV_COLS), F32)
        for hh in range(N_KV):
            kb = kx_ref[hh, pl.ds(r0, 2 * BLOCK), :]
            vb = vx_ref[hh, pl.ds(r0, 2 * BLOCK), :]
            s = lax.dot_general(qs, kb, (((1,), (1,)), ((), ())), preferred_element_type=F32)
            s = jnp.where(mask, s, NEG_INF)
            sink = jnp.zeros((n_rows, 1), F32)
            for gg in range(GQA):
                sink = jnp.where(row_group == gg, sinks_ref[hh * GQA + gg], sink)
            m = jnp.maximum(jnp.max(s, axis=-1, keepdims=True), sink)
            p = jnp.exp(s - m)
            denom = jnp.sum(p, axis=-1, keepdims=True) + jnp.exp(sink - m)
            pn = (p * (1.0 / denom)).astype(BF16)
            o = o + jnp.dot(pn, vb, preferred_element_type=F32)
        for gg in range(GQA):
            mix_ref[pl.ds(r0, BLOCK), gg * KV_COLS:(gg + 1) * KV_COLS] = o[gg * BLOCK:(gg + 1) * BLOCK].astype(BF16)
        return carry

    lax.fori_loop(0, n_blk, block_body, 0)

    h1_ref[...] = _outproj_ln(mix_ref[...], h_ref[...], wo_ref, g_ref, b_ref)


def _prompt_mixer(sinks, q, k, v, u, h, kmeta, vmeta, umeta, wpool, pscale, wo, ln_g, ln_b, tm):
    bsz, seq = q.shape[0], q.shape[1]
    tile = lambda b, j: (b, j, 0)
    fixed2 = lambda b, j: (0, 0)
    fixed3 = lambda b, j: (0, 0, 0)
    kv_halo = lambda b, j: (b, jnp.maximum(j * (tm // BLOCK) - 1, 0), 0)
    u_halo = lambda b, j: (b, jnp.maximum(j * (tm // N_META) - 1, 0), 0)
    return pl.pallas_call(
        functools.partial(_prompt_mixer_kernel, tm=tm),
        grid=(bsz, seq // tm),
        in_specs=[
            pl.BlockSpec(memory_space=pltpu.SMEM),
            pl.BlockSpec((None, tm, Q_COLS), tile),
            pl.BlockSpec((None, tm, KV_COLS), tile),
            pl.BlockSpec((None, BLOCK, KV_COLS), kv_halo),
            pl.BlockSpec((N_META, KV_COLS), fixed2),
            pl.BlockSpec((None, tm, KV_COLS), tile),
            pl.BlockSpec((None, BLOCK, KV_COLS), kv_halo),
            pl.BlockSpec((N_META, KV_COLS), fixed2),
            pl.BlockSpec((None, tm, POOL_W), tile),
            pl.BlockSpec((None, N_META, POOL_W), u_halo),
            pl.BlockSpec((N_META, POOL_W), fixed2),
            pl.BlockSpec((None, tm, D_MODEL), tile),
            pl.BlockSpec((len(POOL_WINDOWS), POOL_GW, POOL_GW), fixed3),
            pl.BlockSpec((1, POOL_W), fixed2),
            pl.BlockSpec((D_MODEL, D_MODEL), fixed2),
            pl.BlockSpec((1, D_MODEL), fixed2),
            pl.BlockSpec((1, D_MODEL), fixed2),
        ],
        out_specs=pl.BlockSpec((None, tm, D_MODEL), tile),
        out_shape=jax.ShapeDtypeStruct((bsz, seq, D_MODEL), F32),
        scratch_shapes=[
            pltpu.VMEM((N_KV, BLOCK + tm, KV_COLS), BF16),
            pltpu.VMEM((N_KV, BLOCK + tm, KV_COLS), BF16),
            pltpu.VMEM((N_META + tm, POOL_W), F32),
            pltpu.VMEM((tm, D_MODEL), BF16),
        ],
        compiler_params=_params(("parallel", "arbitrary")),
        name="prompt_mixer",
    )(sinks, q, k, k, kmeta, v, v, vmeta, u, u, umeta, h, wpool, pscale, wo, ln_g, ln_b)


def _sample_attn_kernel(sinks_ref, q_ref, kn_ref, vn_ref, ck_ref, cv_ref,
                        attn_ref, nk_ref, nv_ref):
    bb = q_ref.shape[0]
    wk = ck_ref.shape[1]
    ck = ck_ref[...]
    cv = cv_ref[...]
    kn = kn_ref[...]
    vn = vn_ref[...]
    head_of_lane = lax.broadcasted_iota(jnp.int32, (KV_COLS, 128), 0) // HEAD_DIM
    out_lane = lax.broadcasted_iota(jnp.int32, (KV_COLS, 128), 1)
    seg = (head_of_lane == out_lane).astype(BF16)
    lane_t = lax.broadcasted_iota(jnp.int32, (128, KV_COLS), 0)
    head_t = lax.broadcasted_iota(jnp.int32, (128, KV_COLS), 1) // HEAD_DIM
    spread = (lane_t == head_t).astype(BF16)
    lane = lax.broadcasted_iota(jnp.int32, (1, 128), 1)

    for gg in range(GQA):
        qg = q_ref[:, gg * KV_COLS:(gg + 1) * KV_COLS].astype(F32)
        prod = (ck * qg[:, None, :]).astype(BF16).reshape(bb * wk, KV_COLS)
        s = jnp.dot(prod, seg, preferred_element_type=F32).reshape(bb, wk, 128)
        s_new = jnp.dot((kn * qg).astype(BF16), seg, preferred_element_type=F32)
        sink = jnp.zeros((1, 128), F32)
        for hh in range(N_KV):
            sink = jnp.where(lane == hh, sinks_ref[hh * GQA + gg], sink)
        m = jnp.maximum(jnp.maximum(jnp.max(s, axis=1), s_new), sink)
        p = jnp.exp(s - m[:, None, :])
        p_new = jnp.exp(s_new - m)
        denom = jnp.sum(p, axis=1) + p_new + jnp.exp(sink - m)
        r = 1.0 / denom
        pn = (p * r[:, None, :]).astype(BF16).reshape(bb * wk, 128)
        pn_new = (p_new * r).astype(BF16)
        pe = jnp.dot(pn, spread, preferred_element_type=F32).reshape(bb, wk, KV_COLS)
        pe_new = jnp.dot(pn_new, spread, preferred_element_type=F32)
        o = jnp.sum(pe * cv, axis=1) + pe_new * vn
        attn_ref[:, gg * KV_COLS:(gg + 1) * KV_COLS] = o.astype(BF16)

    nk_ref[:, 0:wk - 1, :] = ck_ref[:, 1:wk, :]
    nk_ref[:, wk - 1:wk, :] = kn[:, None, :]
    nv_ref[:, 0:wk - 1, :] = cv_ref[:, 1:wk, :]
    nv_ref[:, wk - 1:wk, :] = vn[:, None, :]


def _sample_attn(sinks, q, kn, vn, ck, cv, bb):
    bsz, wk = ck.shape[0], ck.shape[1]
    row = lambda i: (i, 0)
    cache = lambda i: (i, 0, 0)
    return pl.pallas_call(
        _sample_attn_kernel,
        grid=(bsz // bb,),
        in_specs=[
            pl.BlockSpec(memory_space=pltpu.SMEM),
            pl.BlockSpec((bb, Q_COLS), row),
            pl.BlockSpec((bb, KV_COLS), row),
            pl.BlockSpec((bb, KV_COLS), row),
            pl.BlockSpec((bb, wk, KV_COLS), cache),
            pl.BlockSpec((bb, wk, KV_COLS), cache),
        ],
        out_specs=[
            pl.BlockSpec((bb, Q_COLS), row),
            pl.BlockSpec((bb, wk, KV_COLS), cache),
            pl.BlockSpec((bb, wk, KV_COLS), cache),
        ],
        out_shape=[
            jax.ShapeDtypeStruct((bsz, Q_COLS), BF16),
            jax.ShapeDtypeStruct((bsz, wk, KV_COLS), F32),
            jax.ShapeDtypeStruct((bsz, wk, KV_COLS), F32),
        ],
        compiler_params=_params(("parallel",)),
        name="sample_attn",
    )(sinks, q, kn, vn, ck, cv)


def _sample_mix_kernel(attn_ref, u_ref, st_ref, h_ref, wpool_ref, pscale_ref, wo_ref, g_ref, b_ref,
                       h1_ref, nst_ref, mix_ref):
    u = u_ref[...]
    mix_ref[:, 0:Q_COLS] = attn_ref[...]
    for g, w in enumerate(POOL_WINDOWS):
        cs = slice(g * POOL_GW, (g + 1) * POOL_GW)
        acc = u[:, cs]
        if w > 1:
            acc = acc + jnp.sum(st_ref[:, POOL_HIST - (w - 1):POOL_HIST, cs], axis=1)
        d = acc * (1.0 / w) - u[:, cs]
        mix_ref[:, Q_COLS + g * POOL_GW:Q_COLS + (g + 1) * POOL_GW] = _pool_group_matmul(d, g, wpool_ref, pscale_ref)
    nst_ref[:, 0:POOL_HIST - 1, :] = st_ref[:, 1:POOL_HIST, :]
    nst_ref[:, POOL_HIST - 1:POOL_HIST, :] = u[:, None, :]
    h1_ref[...] = _outproj_ln(mix_ref[...], h_ref[...], wo_ref, g_ref, b_ref)


def _sample_mix(attn, u, state, h, wpool, pscale, wo, ln_g, ln_b):
    bsz = attn.shape[0]
    return pl.pallas_call(
        _sample_mix_kernel,
        out_shape=[
            jax.ShapeDtypeStruct((bsz, D_MODEL), F32),
            jax.ShapeDtypeStruct((bsz, POOL_HIST, POOL_W), F32),
        ],
        scratch_shapes=[pltpu.VMEM((bsz, D_MODEL), BF16)],
        compiler_params=pltpu.CompilerParams(vmem_limit_bytes=V7X_VMEM_LIMIT),
        name="sample_mix",
    )(attn, u, state, h, wpool, pscale, wo, ln_g, ln_b)


def _ffn_kernel(h_ref, wg_ref, wu_ref, wd_ref, g_ref, b_ref, o_ref, hb_ref):
    f = pl.program_id(1)

    @pl.when(f == 0)
    def _():
        hb_ref[...] = h_ref[...].astype(BF16)

    hb = hb_ref[...]
    gate = jnp.dot(hb, wg_ref[...], preferred_element_type=F32)
    up = jnp.dot(hb, wu_ref[...], preferred_element_type=F32)
    a = (gate * (1.0 / (1.0 + jnp.exp(-gate))) * up).astype(BF16)
    part = jnp.dot(a, wd_ref[...], preferred_element_type=F32)

    @pl.when(f == 0)
    def _():
        o_ref[...] = part

    @pl.when(f > 0)
    def _():
        o_ref[...] += part

    @pl.when(f == pl.num_programs(1) - 1)
    def _():
        o_ref[...] = _layer_norm(ALPHA * h_ref[...] + o_ref[...], g_ref[...], b_ref[...])


def _ffn(h1, wg, wu, wd, ln_g, ln_b, tm, tf):
    rows = h1.shape[0]
    return pl.pallas_call(
        _ffn_kernel,
        grid=(rows // tm, D_FF // tf),
        in_specs=[
            pl.BlockSpec((tm, D_MODEL), lambda i, f: (i, 0)),
            pl.BlockSpec((D_MODEL, tf), lambda i, f: (0, f)),
            pl.BlockSpec((D_MODEL, tf), lambda i, f: (0, f)),
            pl.BlockSpec((tf, D_MODEL), lambda i, f: (f, 0)),
            pl.BlockSpec((1, D_MODEL), lambda i, f: (0, 0)),
            pl.BlockSpec((1, D_MODEL), lambda i, f: (0, 0)),
        ],
        out_specs=pl.BlockSpec((tm, D_MODEL), lambda i, f: (i, 0)),
        out_shape=jax.ShapeDtypeStruct((rows, D_MODEL), F32),
        scratch_shapes=[pltpu.VMEM((tm, D_MODEL), BF16)],
        compiler_params=_params(("parallel", "arbitrary")),
        name="ffn",
    )(h1, wg, wu, wd, ln_g, ln_b)


def _rope_tables(pos):
    half = ROT_DIM // 2
    inv_freq = ROPE_THETA ** (-jnp.arange(half, dtype=F32) * 2.0 / ROT_DIM)
    ang = pos.astype(F32)[:, None] * inv_freq
    cos, sin = jnp.cos(ang), jnp.sin(ang)
    t = pos.shape[0]
    ones = jnp.ones((t, HEAD_DIM - ROT_DIM), F32)
    zeros = jnp.zeros((t, HEAD_DIM - ROT_DIM), F32)
    zh = jnp.zeros((t, half), F32)
    c = jnp.concatenate([cos, cos, ones], -1)
    s_lo = jnp.concatenate([-sin, zh, zeros], -1)
    s_hi = jnp.concatenate([zh, sin, zeros], -1)
    return tuple(jnp.concatenate([a, a], -1) for a in (c, s_lo, s_hi))


def kernel(x_prompt, x_sample, cache_k, cache_v, state_pool, meta_tokens, ln_in_g, ln_in_b, w_in, b_in,
           attn_sinks, w_pool, pool_scale, w_o, ln1_g, ln1_b, w_gate, w_up, w_down, ln2_g, ln2_b):
    bp, seq, _ = x_prompt.shape
    bs = x_sample.shape[0]
    wk = cache_k.shape[2]
    assert x_sample.shape[1] == 1 and cache_k.shape[0] == 1

    def q_perm_cols(a):
        lead = a.shape[:-1]
        return a.reshape(lead + (N_KV, GQA, HEAD_DIM)).swapaxes(-3, -2).reshape(lead + (Q_COLS,))

    w_in0, b_in0, w_o0 = w_in[0], b_in[0], w_o[0]
    w_in_p = jnp.concatenate([q_perm_cols(w_in0[:, :Q_COLS]), w_in0[:, Q_COLS:]], -1).astype(BF16)
    b_in_p = jnp.concatenate([q_perm_cols(b_in0[:Q_COLS]), b_in0[Q_COLS:]], -1)[None]
    w_o_attn = w_o0[:Q_COLS].reshape(N_KV, GQA, HEAD_DIM, D_MODEL).swapaxes(0, 1).reshape(Q_COLS, D_MODEL)
    w_o_p = jnp.concatenate([w_o_attn, w_o0[Q_COLS:]], 0).astype(BF16)
    wpool = w_pool[0].astype(BF16)
    pscale = pool_scale[0][None]
    wg, wu, wd = w_gate[0].astype(BF16), w_up[0].astype(BF16), w_down[0].astype(BF16)
    sinks = attn_sinks[0]
    g_in, b_ln_in = ln_in_g[None], ln_in_b[None]
    g1, b1, g2, b2 = ln1_g[0][None], ln1_b[0][None], ln2_g[0][None], ln2_b[0][None]

    tabs_meta = _rope_tables(jnp.arange(N_META, dtype=jnp.int32))
    tabs_prompt = _rope_tables(N_META + jnp.arange(seq, dtype=jnp.int32))
    tabs_sample = _rope_tables(jnp.full((bs,), PAST_LEN, dtype=jnp.int32))

    _, _, k_meta, v_meta, u_meta = _inproj(meta_tokens, g_in, b_ln_in, w_in_p, b_in_p, tabs_meta, N_META)

    tm_in = 512
    h_p, q_p, k_p, v_p, u_p = _inproj(x_prompt.reshape(bp * seq, D_MODEL), g_in, b_ln_in, w_in_p, b_in_p,
                                      tabs_prompt, tm_in)
    k_p3 = k_p.reshape(bp, seq, KV_COLS)
    v_p3 = v_p.reshape(bp, seq, KV_COLS)
    u_p3 = u_p.reshape(bp, seq, POOL_W)
    h1_p = _prompt_mixer(sinks, q_p.reshape(bp, seq, Q_COLS), k_p3, v_p3, u_p3, h_p.reshape(bp, seq, D_MODEL),
                         k_meta, v_meta, u_meta, wpool, pscale, w_o_p, g1, b1, 512)
    y_prompt = _ffn(h1_p.reshape(bp * seq, D_MODEL), wg, wu, wd, g2, b2, 512, 512).reshape(bp, seq, D_MODEL)

    h_s, q_s, k_s, v_s, u_s = _inproj(x_sample.reshape(bs, D_MODEL), g_in, b_ln_in, w_in_p, b_in_p,
                                      tabs_sample, bs)
    attn_s, nk_s, nv_s = _sample_attn(sinks, q_s, k_s, v_s, cache_k.reshape(bs, wk, KV_COLS),
                                      cache_v.reshape(bs, wk, KV_COLS), 16)
    h1_s, npool_s = _sample_mix(attn_s, u_s, state_pool[0], h_s, wpool, pscale, w_o_p, g1, b1)
    y_sample = _ffn(h1_s, wg, wu, wd, g2, b2, bs, 512).reshape(bs, 1, D_MODEL)

    new_k_prompt = k_p3[:, seq - WINDOW:].reshape(1, bp, WINDOW, N_KV, HEAD_DIM)
    new_v_prompt = v_p3[:, seq - WINDOW:].reshape(1, bp, WINDOW, N_KV, HEAD_DIM)
    new_pool_prompt = u_p3[:, seq - POOL_HIST:][None]
    return (y_prompt, y_sample, new_k_prompt, new_v_prompt, new_pool_prompt,
            nk_s.reshape(1, bs, wk, N_KV, HEAD_DIM), nv_s.reshape(1, bs, wk, N_KV, HEAD_DIM), npool_s[None])
```

```python
import functools
import math

import jax
import jax.numpy as jnp
from jax import lax
from jax.experimental import pallas as pl
from jax.experimental.pallas import tpu as pltpu

D_MODEL = 2048
N_META = 16
HEAD_DIM = 64
N_KV = 4
GQA = 4
Q_COLS = N_KV * GQA * HEAD_DIM
KV_COLS = N_KV * HEAD_DIM
POOL_W = 1024
POOL_WINDOWS = (2, 4, 8, 16)
POOL_GW = POOL_W // len(POOL_WINDOWS)
POOL_HIST = 15
IN_COLS = Q_COLS + 2 * KV_COLS + POOL_W
D_FF = 5632
WINDOW = 128
BLOCK = 128
ROT_DIM = 16
ROPE_THETA = 500000.0
PAST_LEN = 16384
ALPHA = 2.0 ** 0.25
LN_EPS = 1e-5
NEG_INF = -1e30
QK_SCALE = 1.0 / math.sqrt(HEAD_DIM)
V7X_MXU_COLS = 256
FFN_SUB = V7X_MXU_COLS

V7X_VMEM_LIMIT = 56 * 1024 * 1024

BF16 = jnp.bfloat16
F32 = jnp.float32


def _layer_norm(x, g, b):
    mu = jnp.mean(x, axis=-1, keepdims=True)
    xc = x - mu
    var = jnp.mean(xc * xc, axis=-1, keepdims=True)
    return xc * lax.rsqrt(var + LN_EPS) * g + b


def _params(sem):
    return pltpu.CompilerParams(dimension_semantics=sem, vmem_limit_bytes=V7X_VMEM_LIMIT)


def _inproj_kernel(x_ref, g_ref, b_ref, w_ref, bias_ref, c_ref, s1_ref, s2_ref,
                   h_ref, q_ref, k_ref, v_ref, u_ref):
    h = _layer_norm(x_ref[...], g_ref[...], b_ref[...])
    h_ref[...] = h
    z = jnp.dot(h.astype(BF16), w_ref[...], preferred_element_type=F32) + bias_ref[...]
    cos = c_ref[...]
    s_lo = s1_ref[...]
    s_hi = s2_ref[...]

    def rope(zc):
        return zc * cos + pltpu.roll(zc, 128 - ROT_DIM // 2, 1) * s_lo + pltpu.roll(zc, ROT_DIM // 2, 1) * s_hi

    for c in range(Q_COLS // 128):
        zc = z[:, c * 128:(c + 1) * 128]
        q_ref[:, c * 128:(c + 1) * 128] = (rope(zc) * QK_SCALE).astype(BF16)
    for c in range(KV_COLS // 128):
        zc = z[:, Q_COLS + c * 128:Q_COLS + (c + 1) * 128]
        k_ref[:, c * 128:(c + 1) * 128] = rope(zc)
    v_ref[...] = z[:, Q_COLS + KV_COLS:Q_COLS + 2 * KV_COLS]
    u_ref[...] = z[:, Q_COLS + 2 * KV_COLS:]


def _inproj(x, ln_g, ln_b, w_in, b_in, tabs, tm):
    rows = x.shape[0]
    t_rows = tabs[0].shape[0]
    n_t = t_rows // tm
    row = lambda i: (i, 0)
    fixed = lambda i: (0, 0)
    tab = lambda i: (i % n_t, 0)
    return pl.pallas_call(
        _inproj_kernel,
        grid=(rows // tm,),
        in_specs=[
            pl.BlockSpec((tm, D_MODEL), row),
            pl.BlockSpec((1, D_MODEL), fixed),
            pl.BlockSpec((1, D_MODEL), fixed),
            pl.BlockSpec((D_MODEL, IN_COLS), fixed),
            pl.BlockSpec((1, IN_COLS), fixed),
            pl.BlockSpec((tm, 128), tab),
            pl.BlockSpec((tm, 128), tab),
            pl.BlockSpec((tm, 128), tab),
        ],
        out_specs=[
            pl.BlockSpec((tm, D_MODEL), row),
            pl.BlockSpec((tm, Q_COLS), row),
            pl.BlockSpec((tm, KV_COLS), row),
            pl.BlockSpec((tm, KV_COLS), row),
            pl.BlockSpec((tm, POOL_W), row),
        ],
        out_shape=[
            jax.ShapeDtypeStruct((rows, D_MODEL), F32),
            jax.ShapeDtypeStruct((rows, Q_COLS), BF16),
            jax.ShapeDtypeStruct((rows, KV_COLS), F32),
            jax.ShapeDtypeStruct((rows, KV_COLS), F32),
            jax.ShapeDtypeStruct((rows, POOL_W), F32),
        ],
        compiler_params=_params(("parallel",)),
        name="inproj",
    )(x, ln_g, ln_b, w_in, b_in, *tabs)


def _outproj_ln(mix_bf16, h, wo_ref, g_ref, b_ref):
    mix = jnp.dot(mix_bf16, wo_ref[...], preferred_element_type=F32)
    return _layer_norm(ALPHA * h + mix, g_ref[...], b_ref[...])


def _pool_group_matmul(d, g, wpool_ref, pscale_ref):
    cs = slice(g * POOL_GW, (g + 1) * POOL_GW)
    y = jnp.dot(d.astype(BF16), wpool_ref[g], preferred_element_type=F32)
    return (y * pscale_ref[:, cs]).astype(BF16)


def _prompt_mixer_kernel(sinks_ref, q_ref, k_ref, khalo_ref, kmeta_ref, v_ref, vhalo_ref, vmeta_ref,
                         u_ref, uhalo_ref, umeta_ref, h_ref, wpool_ref, pscale_ref, wo_ref, g_ref, b_ref,
                         h1_ref, kx_ref, vx_ref, ux_ref, *, tm):
    j = pl.program_id(1)
    n_blk = tm // BLOCK
    lane_head = lax.broadcasted_iota(jnp.int32, (1, KV_COLS), 1) // HEAD_DIM

    def put_kv(dst_ref, rows, val):
        for hh in range(N_KV):
            dst_ref[hh, rows, :] = jnp.where(lane_head == hh, val, 0.0).astype(BF16)

    pad = BLOCK - N_META

    @pl.when(j == 0)
    def _():
        zeros = jnp.zeros((pad, KV_COLS), F32)
        put_kv(kx_ref, slice(0, pad), zeros)
        put_kv(vx_ref, slice(0, pad), zeros)
        put_kv(kx_ref, slice(pad, BLOCK), kmeta_ref[...])
        put_kv(vx_ref, slice(pad, BLOCK), vmeta_ref[...])
        ux_ref[0:N_META, :] = umeta_ref[...]

    @pl.when(j > 0)
    def _():
        put_kv(kx_ref, slice(0, BLOCK), khalo_ref[...])
        put_kv(vx_ref, slice(0, BLOCK), vhalo_ref[...])
        ux_ref[0:N_META, :] = uhalo_ref[...]

    put_kv(kx_ref, slice(BLOCK, BLOCK + tm), k_ref[...])
    put_kv(vx_ref, slice(BLOCK, BLOCK + tm), v_ref[...])
    ux_ref[N_META:, :] = u_ref[...]

    n_rows = GQA * BLOCK
    qi = lax.broadcasted_iota(jnp.int32, (n_rows, 2 * BLOCK), 0) % BLOCK
    ki = lax.broadcasted_iota(jnp.int32, (n_rows, 2 * BLOCK), 1)
    band = (ki >= qi) & (ki <= qi + WINDOW)
    row_group = lax.broadcasted_iota(jnp.int32, (n_rows, 1), 0) // BLOCK
    sink_cols = []
    for hh in range(N_KV):
        sink = jnp.zeros((n_rows, 1), F32)
        for gg in range(GQA):
            sink = jnp.where(row_group == gg, sinks_ref[hh * GQA + gg], sink)
        sink_cols.append(sink)

    def mixer_block(blk):
        r0 = blk * BLOCK
        rows = slice(r0, r0 + BLOCK)

        e = ux_ref[r0:r0 + N_META + BLOCK, :]
        sums = []
        cur = e
        for step in (1, 2, 4, 8):
            cur = cur + pltpu.roll(cur, step, 0)
            sums.append(cur[:, :POOL_GW])
            cur = cur[:, POOL_GW:]
        pooled = []
        for g, w in enumerate(POOL_WINDOWS):
            cs = slice(g * POOL_GW, (g + 1) * POOL_GW)
            d = sums[g][N_META:] * (1.0 / w) - e[N_META:, cs]
            pooled.append(_pool_group_matmul(d, g, wpool_ref, pscale_ref))

        if blk == 0:
            mask = band & (ki >= jnp.where(j == 0, pad, 0))
        else:
            mask = band
        qs = jnp.concatenate([q_ref[rows, gg * KV_COLS:(gg + 1) * KV_COLS] for gg in range(GQA)], axis=0)
        o = jnp.zeros((n_rows, KV_COLS), F32)
        for hh in range(N_KV):
            kb = kx_ref[hh, r0:r0 + 2 * BLOCK, :]
            vb = vx_ref[hh, r0:r0 + 2 * BLOCK, :]
            s = lax.dot_general(qs, kb, (((1,), (1,)), ((), ())), preferred_element_type=F32)
            s = jnp.where(mask, s, NEG_INF)
            sink = sink_cols[hh]
            m = jnp.maximum(jnp.max(s, axis=-1, keepdims=True), sink)
            p = jnp.exp(s - m)
            denom = jnp.sum(p, axis=-1, keepdims=True) + jnp.exp(sink - m)
            pn = (p * (1.0 / denom)).astype(BF16)
            o = o + jnp.dot(pn, vb, preferred_element_type=F32)
        attn = [o[gg * BLOCK:(gg + 1) * BLOCK].astype(BF16) for gg in range(GQA)]
        return jnp.concatenate(attn + pooled, axis=1)

    mix = mixer_block(0)
    for blk in range(n_blk):
        nxt = mixer_block(blk + 1) if blk + 1 < n_blk else None
        rows = slice(blk * BLOCK, (blk + 1) * BLOCK)
        h1_ref[rows, :] = _outproj_ln(mix, h_ref[rows, :], wo_ref, g_ref, b_ref)
        mix = nxt


def _prompt_mixer(sinks, q, k, v, u, h, kmeta, vmeta, umeta, wpool, pscale, wo, ln_g, ln_b, tm):
    bsz, seq = q.shape[0], q.shape[1]
    tile = lambda b, j: (b, j, 0)
    fixed2 = lambda b, j: (0, 0)
    fixed3 = lambda b, j: (0, 0, 0)
    kv_halo = lambda b, j: (b, jnp.maximum(j * (tm // BLOCK) - 1, 0), 0)
    u_halo = lambda b, j: (b, jnp.maximum(j * (tm // N_META) - 1, 0), 0)
    return pl.pallas_call(
        functools.partial(_prompt_mixer_kernel, tm=tm),
        grid=(bsz, seq // tm),
        in_specs=[
            pl.BlockSpec(memory_space=pltpu.SMEM),
            pl.BlockSpec((None, tm, Q_COLS), tile),
            pl.BlockSpec((None, tm, KV_COLS), tile),
            pl.BlockSpec((None, BLOCK, KV_COLS), kv_halo),
            pl.BlockSpec((N_META, KV_COLS), fixed2),
            pl.BlockSpec((None, tm, KV_COLS), tile),
            pl.BlockSpec((None, BLOCK, KV_COLS), kv_halo),
            pl.BlockSpec((N_META, KV_COLS), fixed2),
            pl.BlockSpec((None, tm, POOL_W), tile),
            pl.BlockSpec((None, N_META, POOL_W), u_halo),
            pl.BlockSpec((N_META, POOL_W), fixed2),
            pl.BlockSpec((None, tm, D_MODEL), tile),
            pl.BlockSpec((len(POOL_WINDOWS), POOL_GW, POOL_GW), fixed3),
            pl.BlockSpec((1, POOL_W), fixed2),
            pl.BlockSpec((D_MODEL, D_MODEL), fixed2),
            pl.BlockSpec((1, D_MODEL), fixed2),
            pl.BlockSpec((1, D_MODEL), fixed2),
        ],
        out_specs=pl.BlockSpec((None, tm, D_MODEL), tile),
        out_shape=jax.ShapeDtypeStruct((bsz, seq, D_MODEL), F32),
        scratch_shapes=[
            pltpu.VMEM((N_KV, BLOCK + tm, KV_COLS), BF16),
            pltpu.VMEM((N_KV, BLOCK + tm, KV_COLS), BF16),
            pltpu.VMEM((N_META + tm, POOL_W), F32),
        ],
        compiler_params=_params(("parallel", "arbitrary")),
        name="prompt_mixer",
    )(sinks, q, k, k, kmeta, v, v, vmeta, u, u, umeta, h, wpool, pscale, wo, ln_g, ln_b)


def _sample_attn_kernel(sinks_ref, q_ref, kn_ref, vn_ref, ck_ref, cv_ref,
                        attn_ref, nk_ref, nv_ref):
    bb = q_ref.shape[0]
    wk = ck_ref.shape[1]
    ck = ck_ref[...]
    cv = cv_ref[...]
    kn = kn_ref[...]
    vn = vn_ref[...]
    head_of_lane = lax.broadcasted_iota(jnp.int32, (KV_COLS, 128), 0) // HEAD_DIM
    out_lane = lax.broadcasted_iota(jnp.int32, (KV_COLS, 128), 1)
    seg = (head_of_lane == out_lane).astype(BF16)
    lane_t = lax.broadcasted_iota(jnp.int32, (128, KV_COLS), 0)
    head_t = lax.broadcasted_iota(jnp.int32, (128, KV_COLS), 1) // HEAD_DIM
    spread = (lane_t == head_t).astype(BF16)
    lane = lax.broadcasted_iota(jnp.int32, (1, 128), 1)

    for gg in range(GQA):
        qg = q_ref[:, gg * KV_COLS:(gg + 1) * KV_COLS].astype(F32)
        prod = (ck * qg[:, None, :]).astype(BF16).reshape(bb * wk, KV_COLS)
        s = jnp.dot(prod, seg, preferred_element_type=F32).reshape(bb, wk, 128)
        s_new = jnp.dot((kn * qg).astype(BF16), seg, preferred_element_type=F32)
        sink = jnp.zeros((1, 128), F32)
        for hh in range(N_KV):
            sink = jnp.where(lane == hh, sinks_ref[hh * GQA + gg], sink)
        m = jnp.maximum(jnp.maximum(jnp.max(s, axis=1), s_new), sink)
        p = jnp.exp(s - m[:, None, :])
        p_new = jnp.exp(s_new - m)
        denom = jnp.sum(p, axis=1) + p_new + jnp.exp(sink - m)
        r = 1.0 / denom
        pn = (p * r[:, None, :]).astype(BF16).reshape(bb * wk, 128)
        pn_new = (p_new * r).astype(BF16)
        pe = jnp.dot(pn, spread, preferred_element_type=F32).reshape(bb, wk, KV_COLS)
        pe_new = jnp.dot(pn_new, spread, preferred_element_type=F32)
        o = jnp.sum(pe * cv, axis=1) + pe_new * vn
        attn_ref[:, gg * KV_COLS:(gg + 1) * KV_COLS] = o.astype(BF16)

    nk_ref[:, 0:wk - 1, :] = ck_ref[:, 1:wk, :]
    nk_ref[:, wk - 1:wk, :] = kn[:, None, :]
    nv_ref[:, 0:wk - 1, :] = cv_ref[:, 1:wk, :]
    nv_ref[:, wk - 1:wk, :] = vn[:, None, :]


def _sample_attn(sinks, q, kn, vn, ck, cv, bb):
    bsz, wk = ck.shape[0], ck.shape[1]
    row = lambda i: (i, 0)
    cache = lambda i: (i, 0, 0)
    return pl.pallas_call(
        _sample_attn_kernel,
        grid=(bsz // bb,),
        in_specs=[
            pl.BlockSpec(memory_space=pltpu.SMEM),
            pl.BlockSpec((bb, Q_COLS), row),
            pl.BlockSpec((bb, KV_COLS), row),
            pl.BlockSpec((bb, KV_COLS), row),
            pl.BlockSpec((bb, wk, KV_COLS), cache),
            pl.BlockSpec((bb, wk, KV_COLS), cache),
        ],
        out_specs=[
            pl.BlockSpec((bb, Q_COLS), row),
            pl.BlockSpec((bb, wk, KV_COLS), cache),
            pl.BlockSpec((bb, wk, KV_COLS), cache),
        ],
        out_shape=[
            jax.ShapeDtypeStruct((bsz, Q_COLS), BF16),
            jax.ShapeDtypeStruct((bsz, wk, KV_COLS), F32),
            jax.ShapeDtypeStruct((bsz, wk, KV_COLS), F32),
        ],
        compiler_params=_params(("parallel",)),
        name="sample_attn",
    )(sinks, q, kn, vn, ck, cv)


def _sample_mix_kernel(attn_ref, u_ref, st_ref, h_ref, wpool_ref, pscale_ref, wo_ref, g_ref, b_ref,
                       h1_ref, nst_ref, mix_ref):
    u = u_ref[...]
    mix_ref[:, 0:Q_COLS] = attn_ref[...]
    for g, w in enumerate(POOL_WINDOWS):
        cs = slice(g * POOL_GW, (g + 1) * POOL_GW)
        acc = u[:, cs]
        if w > 1:
            acc = acc + jnp.sum(st_ref[:, POOL_HIST - (w - 1):POOL_HIST, cs], axis=1)
        d = acc * (1.0 / w) - u[:, cs]
        mix_ref[:, Q_COLS + g * POOL_GW:Q_COLS + (g + 1) * POOL_GW] = _pool_group_matmul(d, g, wpool_ref, pscale_ref)
    nst_ref[:, 0:POOL_HIST - 1, :] = st_ref[:, 1:POOL_HIST, :]
    nst_ref[:, POOL_HIST - 1:POOL_HIST, :] = u[:, None, :]
    h1_ref[...] = _outproj_ln(mix_ref[...], h_ref[...], wo_ref, g_ref, b_ref)


def _sample_mix(attn, u, state, h, wpool, pscale, wo, ln_g, ln_b):
    bsz = attn.shape[0]
    return pl.pallas_call(
        _sample_mix_kernel,
        out_shape=[
            jax.ShapeDtypeStruct((bsz, D_MODEL), F32),
            jax.ShapeDtypeStruct((bsz, POOL_HIST, POOL_W), F32),
        ],
        scratch_shapes=[pltpu.VMEM((bsz, D_MODEL), BF16)],
        compiler_params=pltpu.CompilerParams(vmem_limit_bytes=V7X_VMEM_LIMIT),
        name="sample_mix",
    )(attn, u, state, h, wpool, pscale, wo, ln_g, ln_b)


def _ffn_kernel(h_ref, wg_ref, wu_ref, wd_ref, g_ref, b_ref, o_ref, hb_ref):
    f = pl.program_id(1)

    @pl.when(f == 0)
    def _():
        h = h_ref[...]
        hb_ref[...] = h.astype(BF16)
        o_ref[...] = ALPHA * h

    hb = hb_ref[...]
    tf = wg_ref.shape[1]
    part = None
    for c in range(tf // FFN_SUB):
        cs = slice(c * FFN_SUB, (c + 1) * FFN_SUB)
        gate = jnp.dot(hb, wg_ref[:, cs], preferred_element_type=F32)
        up = jnp.dot(hb, wu_ref[:, cs], preferred_element_type=F32)
        a = (gate * (1.0 / (1.0 + jnp.exp(-gate))) * up).astype(BF16)
        p = jnp.dot(a, wd_ref[cs, :], preferred_element_type=F32)
        part = p if part is None else part + p
    o_ref[...] += part

    @pl.when(f == pl.num_programs(1) - 1)
    def _():
        o_ref[...] = _layer_norm(o_ref[...], g_ref[...], b_ref[...])


def _ffn(h1, wg, wu, wd, ln_g, ln_b, tm, tf):
    rows = h1.shape[0]
    return pl.pallas_call(
        _ffn_kernel,
        grid=(rows // tm, D_FF // tf),
        in_specs=[
            pl.BlockSpec((tm, D_MODEL), lambda i, f: (i, 0)),
            pl.BlockSpec((D_MODEL, tf), lambda i, f: (0, f)),
            pl.BlockSpec((D_MODEL, tf), lambda i, f: (0, f)),
            pl.BlockSpec((tf, D_MODEL), lambda i, f: (f, 0)),
            pl.BlockSpec((1, D_MODEL), lambda i, f: (0, 0)),
            pl.BlockSpec((1, D_MODEL), lambda i, f: (0, 0)),
        ],
        out_specs=pl.BlockSpec((tm, D_MODEL), lambda i, f: (i, 0)),
        out_shape=jax.ShapeDtypeStruct((rows, D_MODEL), F32),
        scratch_shapes=[pltpu.VMEM((tm, D_MODEL), BF16)],
        compiler_params=_params(("parallel", "arbitrary")),
        name="ffn",
    )(h1, wg, wu, wd, ln_g, ln_b)


def _rope_tables(pos):
    half = ROT_DIM // 2
    inv_freq = ROPE_THETA ** (-jnp.arange(half, dtype=F32) * 2.0 / ROT_DIM)
    ang = pos.astype(F32)[:, None] * inv_freq
    cos, sin = jnp.cos(ang), jnp.sin(ang)
    t = pos.shape[0]
    ones = jnp.ones((t, HEAD_DIM - ROT_DIM), F32)
    zeros = jnp.zeros((t, HEAD_DIM - ROT_DIM), F32)
    zh = jnp.zeros((t, half), F32)
    c = jnp.concatenate([cos, cos, ones], -1)
    s_lo = jnp.concatenate([-sin, zh, zeros], -1)
    s_hi = jnp.concatenate([zh, sin, zeros], -1)
    return tuple(jnp.concatenate([a, a], -1) for a in (c, s_lo, s_hi))


def kernel(x_prompt, x_sample, cache_k, cache_v, state_pool, meta_tokens, ln_in_g, ln_in_b, w_in, b_in,
           attn_sinks, w_pool, pool_scale, w_o, ln1_g, ln1_b, w_gate, w_up, w_down, ln2_g, ln2_b):
    bp, seq, _ = x_prompt.shape
    bs = x_sample.shape[0]
    wk = cache_k.shape[2]
    assert x_sample.shape[1] == 1 and cache_k.shape[0] == 1

    def q_perm_cols(a):
        lead = a.shape[:-1]
        return a.reshape(lead + (N_KV, GQA, HEAD_DIM)).swapaxes(-3, -2).reshape(lead + (Q_COLS,))

    w_in0, b_in0, w_o0 = w_in[0], b_in[0], w_o[0]
    w_in_p = jnp.concatenate([q_perm_cols(w_in0[:, :Q_COLS]), w_in0[:, Q_COLS:]], -1).astype(BF16)
    b_in_p = jnp.concatenate([q_perm_cols(b_in0[:Q_COLS]), b_in0[Q_COLS:]], -1)[None]
    w_o_attn = w_o0[:Q_COLS].reshape(N_KV, GQA, HEAD_DIM, D_MODEL).swapaxes(0, 1).reshape(Q_COLS, D_MODEL)
    w_o_p = jnp.concatenate([w_o_attn, w_o0[Q_COLS:]], 0).astype(BF16)
    wpool = w_pool[0].astype(BF16)
    pscale = pool_scale[0][None]
    wg, wu, wd = w_gate[0].astype(BF16), w_up[0].astype(BF16), w_down[0].astype(BF16)
    sinks = attn_sinks[0]
    g_in, b_ln_in = ln_in_g[None], ln_in_b[None]
    g1, b1, g2, b2 = ln1_g[0][None], ln1_b[0][None], ln2_g[0][None], ln2_b[0][None]

    tabs_meta = _rope_tables(jnp.arange(N_META, dtype=jnp.int32))
    tabs_prompt = _rope_tables(N_META + jnp.arange(seq, dtype=jnp.int32))
    tabs_sample = _rope_tables(jnp.full((bs,), PAST_LEN, dtype=jnp.int32))

    _, _, k_meta, v_meta, u_meta = _inproj(meta_tokens, g_in, b_ln_in, w_in_p, b_in_p, tabs_meta, N_META)

    tm_in = 512
    h_p, q_p, k_p, v_p, u_p = _inproj(x_prompt.reshape(bp * seq, D_MODEL), g_in, b_ln_in, w_in_p, b_in_p,
                                      tabs_prompt, tm_in)
    k_p3 = k_p.reshape(bp, seq, KV_COLS)
    v_p3 = v_p.reshape(bp, seq, KV_COLS)
    u_p3 = u_p.reshape(bp, seq, POOL_W)
    h1_p = _prompt_mixer(sinks, q_p.reshape(bp, seq, Q_COLS), k_p3, v_p3, u_p3, h_p.reshape(bp, seq, D_MODEL),
                         k_meta, v_meta, u_meta, wpool, pscale, w_o_p, g1, b1, 512)
    y_prompt = _ffn(h1_p.reshape(bp * seq, D_MODEL), wg, wu, wd, g2, b2, 512, 512).reshape(bp, seq, D_MODEL)

    h_s, q_s, k_s, v_s, u_s = _inproj(x_sample.reshape(bs, D_MODEL), g_in, b_ln_in, w_in_p, b_in_p,
                                      tabs_sample, bs)
    attn_s, nk_s, nv_s = _sample_attn(sinks, q_s, k_s, v_s, cache_k.reshape(bs, wk, KV_COLS),
                                      cache_v.reshape(bs, wk, KV_COLS), 16)
    h1_s, npool_s = _sample_mix(attn_s, u_s, state_pool[0], h_s, wpool, pscale, w_o_p, g1, b1)
    y_sample = _ffn(h1_s, wg, wu, wd, g2, b2, bs, 512).reshape(bs, 1, D_MODEL)

    new_k_prompt = k_p3[:, seq - WINDOW:].reshape(1, bp, WINDOW, N_KV, HEAD_DIM)
    new_v_prompt = v_p3[:, seq - WINDOW:].reshape(1, bp, WINDOW, N_KV, HEAD_DIM)
    new_pool_prompt = u_p3[:, seq - POOL_HIST:][None]
    return (y_prompt, y_sample, new_k_prompt, new_v_prompt, new_pool_prompt,
            nk_s.reshape(1, bs, wk, N_KV, HEAD_DIM), nv_s.reshape(1, bs, wk, N_KV, HEAD_DIM), npool_s[None])
```

```python
import functools
import math

import jax
import jax.numpy as jnp
from jax import lax
from jax.experimental import pallas as pl
from jax.experimental.pallas import tpu as pltpu

D_MODEL = 2048
N_META = 16
HEAD_DIM = 64
N_KV = 4
GQA = 4
Q_COLS = N_KV * GQA * HEAD_DIM
KV_COLS = N_KV * HEAD_DIM
POOL_W = 1024
POOL_WINDOWS = (2, 4, 8, 16)
POOL_GW = POOL_W // len(POOL_WINDOWS)
POOL_HIST = 15
IN_COLS = Q_COLS + 2 * KV_COLS + POOL_W
D_FF = 5632
WINDOW = 128
BLOCK = 128
ROT_DIM = 16
ROPE_THETA = 500000.0
PAST_LEN = 16384
ALPHA = 2.0 ** 0.25
LN_EPS = 1e-5
NEG_INF = -1e30
LOG2E = math.log2(math.e)
QK_SCALE = LOG2E / math.sqrt(HEAD_DIM)
V7X_MXU_COLS = 256
FFN_SUB = V7X_MXU_COLS
ROW_SUB = 256

V7X_VMEM_LIMIT = 62 * 1024 * 1024

BF16 = jnp.bfloat16
F32 = jnp.float32


def _layer_norm(x, g, b):
    mu = jnp.mean(x, axis=-1, keepdims=True)
    xc = x - mu
    var = jnp.mean(xc * xc, axis=-1, keepdims=True)
    return xc * lax.rsqrt(var + LN_EPS) * g + b


def _params(sem):
    return pltpu.CompilerParams(dimension_semantics=sem, vmem_limit_bytes=V7X_VMEM_LIMIT)


def _inproj_kernel(x_ref, g_ref, b_ref, w_ref, bias_ref, c_ref, s1_ref, s2_ref,
                   h_ref, q_ref, k_ref, v_ref, u_ref):
    tm = x_ref.shape[0]
    sub = min(tm, ROW_SUB)
    for r in range(tm // sub):
        rows = slice(r * sub, (r + 1) * sub)
        h = _layer_norm(x_ref[rows, :], g_ref[...], b_ref[...])
        h_ref[rows, :] = h
        z = jnp.dot(h.astype(BF16), w_ref[...], preferred_element_type=F32) + bias_ref[...]
        cos = c_ref[rows, :]
        s_lo = s1_ref[rows, :]
        s_hi = s2_ref[rows, :]

        def rope(zc):
            return (zc * cos + pltpu.roll(zc, 128 - ROT_DIM // 2, 1) * s_lo
                    + pltpu.roll(zc, ROT_DIM // 2, 1) * s_hi)

        for c in range(Q_COLS // 128):
            zc = z[:, c * 128:(c + 1) * 128]
            q_ref[rows, c * 128:(c + 1) * 128] = (rope(zc) * QK_SCALE).astype(BF16)
        for c in range(KV_COLS // 128):
            zc = z[:, Q_COLS + c * 128:Q_COLS + (c + 1) * 128]
            k_ref[rows, c * 128:(c + 1) * 128] = rope(zc)
        v_ref[rows, :] = z[:, Q_COLS + KV_COLS:Q_COLS + 2 * KV_COLS]
        u_ref[rows, :] = z[:, Q_COLS + 2 * KV_COLS:]


def _inproj(x, ln_g, ln_b, w_in, b_in, tabs, tm):
    rows = x.shape[0]
    t_rows = tabs[0].shape[0]
    n_t = t_rows // tm
    row = lambda i: (i, 0)
    fixed = lambda i: (0, 0)
    tab = lambda i: (i % n_t, 0)
    return pl.pallas_call(
        _inproj_kernel,
        grid=(rows // tm,),
        in_specs=[
            pl.BlockSpec((tm, D_MODEL), row),
            pl.BlockSpec((1, D_MODEL), fixed),
            pl.BlockSpec((1, D_MODEL), fixed),
            pl.BlockSpec((D_MODEL, IN_COLS), fixed),
            pl.BlockSpec((1, IN_COLS), fixed),
            pl.BlockSpec((tm, 128), tab),
            pl.BlockSpec((tm, 128), tab),
            pl.BlockSpec((tm, 128), tab),
        ],
        out_specs=[
            pl.BlockSpec((tm, D_MODEL), row),
            pl.BlockSpec((tm, Q_COLS), row),
            pl.BlockSpec((tm, KV_COLS), row),
            pl.BlockSpec((tm, KV_COLS), row),
            pl.BlockSpec((tm, POOL_W), row),
        ],
        out_shape=[
            jax.ShapeDtypeStruct((rows, D_MODEL), F32),
            jax.ShapeDtypeStruct((rows, Q_COLS), BF16),
            jax.ShapeDtypeStruct((rows, KV_COLS), F32),
            jax.ShapeDtypeStruct((rows, KV_COLS), F32),
            jax.ShapeDtypeStruct((rows, POOL_W), F32),
        ],
        compiler_params=_params(("parallel",)),
        name="inproj",
    )(x, ln_g, ln_b, w_in, b_in, *tabs)


def _outproj_ln(mix_bf16, h, wo_ref, g_ref, b_ref):
    mix = jnp.dot(mix_bf16, wo_ref[...], preferred_element_type=F32)
    return _layer_norm(ALPHA * h + mix, g_ref[...], b_ref[...])


def _pool_group_matmul(d, g, wpool_ref, pscale_ref):
    cs = slice(g * POOL_GW, (g + 1) * POOL_GW)
    y = jnp.dot(d.astype(BF16), wpool_ref[g], preferred_element_type=F32)
    return (y * pscale_ref[:, cs]).astype(BF16)


def _prompt_mixer_kernel(sinks_ref, q_ref, k_ref, khalo_ref, kmeta_ref, v_ref, vhalo_ref, vmeta_ref,
                         u_ref, uhalo_ref, umeta_ref, wpool_ref, pscale_ref,
                         mix_ref, kx_ref, vx_ref, ux_ref, *, tm):
    j = pl.program_id(1)
    n_blk = tm // BLOCK
    lane_head = lax.broadcasted_iota(jnp.int32, (1, KV_COLS), 1) // HEAD_DIM

    def put_kv(dst_ref, rows, val):
        for hh in range(N_KV):
            dst_ref[hh, rows, :] = jnp.where(lane_head == hh, val, 0.0).astype(BF16)

    pad = BLOCK - N_META

    @pl.when(j == 0)
    def _():
        zeros = jnp.zeros((pad, KV_COLS), F32)
        put_kv(kx_ref, slice(0, pad), zeros)
        put_kv(vx_ref, slice(0, pad), zeros)
        put_kv(kx_ref, slice(pad, BLOCK), kmeta_ref[...])
        put_kv(vx_ref, slice(pad, BLOCK), vmeta_ref[...])
        ux_ref[0:N_META, :] = umeta_ref[...]

    @pl.when(j > 0)
    def _():
        put_kv(kx_ref, slice(0, BLOCK), khalo_ref[...])
        put_kv(vx_ref, slice(0, BLOCK), vhalo_ref[...])
        ux_ref[0:N_META, :] = uhalo_ref[...]

    put_kv(kx_ref, slice(BLOCK, BLOCK + tm), k_ref[...])
    put_kv(vx_ref, slice(BLOCK, BLOCK + tm), v_ref[...])
    ux_ref[N_META:, :] = u_ref[...]

    n_rows = GQA * BLOCK
    qi = lax.broadcasted_iota(jnp.int32, (n_rows, 2 * BLOCK), 0) % BLOCK
    ki = lax.broadcasted_iota(jnp.int32, (n_rows, 2 * BLOCK), 1)
    band = (ki >= qi) & (ki <= qi + WINDOW)
    row_group = lax.broadcasted_iota(jnp.int32, (n_rows, 1), 0) // BLOCK
    sink_cols = []
    for hh in range(N_KV):
        sink = jnp.zeros((n_rows, 1), F32)
        for gg in range(GQA):
            sink = jnp.where(row_group == gg, sinks_ref[hh * GQA + gg] * LOG2E, sink)
        sink_cols.append(sink)

    def mixer_block(blk):
        r0 = blk * BLOCK
        rows = slice(r0, r0 + BLOCK)

        e = ux_ref[r0:r0 + N_META + BLOCK, :]
        sums = []
        cur = e
        for step in (1, 2, 4, 8):
            cur = cur + pltpu.roll(cur, step, 0)
            sums.append(cur[:, :POOL_GW])
            cur = cur[:, POOL_GW:]
        pooled = []
        for g, w in enumerate(POOL_WINDOWS):
            cs = slice(g * POOL_GW, (g + 1) * POOL_GW)
            d = sums[g][N_META:] * (1.0 / w) - e[N_META:, cs]
            pooled.append(_pool_group_matmul(d, g, wpool_ref, pscale_ref))

        if blk == 0:
            mask = band & (ki >= jnp.where(j == 0, pad, 0))
        else:
            mask = band
        qs = jnp.concatenate([q_ref[rows, gg * KV_COLS:(gg + 1) * KV_COLS] for gg in range(GQA)], axis=0)
        o = jnp.zeros((n_rows, KV_COLS), F32)
        for hh in range(N_KV):
            kb = kx_ref[hh, r0:r0 + 2 * BLOCK, :]
            vb = vx_ref[hh, r0:r0 + 2 * BLOCK, :]
            s = lax.dot_general(qs, kb, (((1,), (1,)), ((), ())), preferred_element_type=F32)
            s = jnp.where(mask, s, NEG_INF)
            sink = sink_cols[hh]
            m = jnp.max(s, axis=-1, keepdims=True)
            p = jnp.exp2(s - m)
            denom = jnp.sum(p, axis=-1, keepdims=True) + jnp.exp2(sink - m)
            pn = (p * (1.0 / denom)).astype(BF16)
            o = o + jnp.dot(pn, vb, preferred_element_type=F32)
        attn = [o[gg * BLOCK:(gg + 1) * BLOCK].astype(BF16) for gg in range(GQA)]
        return jnp.concatenate(attn + pooled, axis=1)

    for blk in range(n_blk):
        mix_ref[blk * BLOCK:(blk + 1) * BLOCK, :] = mixer_block(blk)


def _prompt_mixer(sinks, q, k, v, u, kmeta, vmeta, umeta, wpool, pscale, tm):
    bsz, seq = q.shape[0], q.shape[1]
    tile = lambda b, j: (b, j, 0)
    fixed2 = lambda b, j: (0, 0)
    fixed3 = lambda b, j: (0, 0, 0)
    kv_halo = lambda b, j: (b, jnp.maximum(j * (tm // BLOCK) - 1, 0), 0)
    u_halo = lambda b, j: (b, jnp.maximum(j * (tm // N_META) - 1, 0), 0)
    return pl.pallas_call(
        functools.partial(_prompt_mixer_kernel, tm=tm),
        grid=(bsz, seq // tm),
        in_specs=[
            pl.BlockSpec(memory_space=pltpu.SMEM),
            pl.BlockSpec((None, tm, Q_COLS), tile),
            pl.BlockSpec((None, tm, KV_COLS), tile),
            pl.BlockSpec((None, BLOCK, KV_COLS), kv_halo),
            pl.BlockSpec((N_META, KV_COLS), fixed2),
            pl.BlockSpec((None, tm, KV_COLS), tile),
            pl.BlockSpec((None, BLOCK, KV_COLS), kv_halo),
            pl.BlockSpec((N_META, KV_COLS), fixed2),
            pl.BlockSpec((None, tm, POOL_W), tile),
            pl.BlockSpec((None, N_META, POOL_W), u_halo),
            pl.BlockSpec((N_META, POOL_W), fixed2),
            pl.BlockSpec((len(POOL_WINDOWS), POOL_GW, POOL_GW), fixed3),
            pl.BlockSpec((1, POOL_W), fixed2),
        ],
        out_specs=pl.BlockSpec((None, tm, D_MODEL), tile),
        out_shape=jax.ShapeDtypeStruct((bsz, seq, D_MODEL), BF16),
        scratch_shapes=[
            pltpu.VMEM((N_KV, BLOCK + tm, KV_COLS), BF16),
            pltpu.VMEM((N_KV, BLOCK + tm, KV_COLS), BF16),
            pltpu.VMEM((N_META + tm, POOL_W), F32),
        ],
        compiler_params=_params(("parallel", "arbitrary")),
        name="prompt_mixer",
    )(sinks, q, k, k, kmeta, v, v, vmeta, u, u, umeta, wpool, pscale)


def _outproj_kernel(mix_ref, h_ref, wo_ref, g_ref, b_ref, h1_ref, wos_ref):
    @pl.when(pl.program_id(0) == 0)
    def _():
        wos_ref[...] = wo_ref[...]

    tm = mix_ref.shape[0]
    sub = min(tm, ROW_SUB)
    for r in range(tm // sub):
        rows = slice(r * sub, (r + 1) * sub)
        h1_ref[rows, :] = _outproj_ln(mix_ref[rows, :], h_ref[rows, :], wos_ref, g_ref, b_ref)


def _outproj(mix, h, wo, ln_g, ln_b, tm):
    rows = mix.shape[0]
    row = lambda i: (i, 0)
    fixed = lambda i: (0, 0)
    return pl.pallas_call(
        _outproj_kernel,
        grid=(rows // tm,),
        in_specs=[
            pl.BlockSpec((tm, D_MODEL), row),
            pl.BlockSpec((tm, D_MODEL), row),
            pl.BlockSpec((D_MODEL, D_MODEL), fixed, pipeline_mode=pl.Buffered(1)),
            pl.BlockSpec((1, D_MODEL), fixed),
            pl.BlockSpec((1, D_MODEL), fixed),
        ],
        out_specs=pl.BlockSpec((tm, D_MODEL), row),
        out_shape=jax.ShapeDtypeStruct((rows, D_MODEL), F32),
        scratch_shapes=[pltpu.VMEM((D_MODEL, D_MODEL), BF16)],
        compiler_params=_params(("arbitrary",)),
        name="outproj",
    )(mix, h, wo, ln_g, ln_b)


def _sample_attn_kernel(sinks_ref, q_ref, kn_ref, vn_ref, ck_ref, cv_ref,
                        attn_ref, nk_ref, nv_ref):
    bb = q_ref.shape[0]
    wk = ck_ref.shape[1]
    ck = ck_ref[...]
    cv = cv_ref[...]
    kn = kn_ref[...]
    vn = vn_ref[...]
    head_of_lane = lax.broadcasted_iota(jnp.int32, (KV_COLS, 128), 0) // HEAD_DIM
    out_lane = lax.broadcasted_iota(jnp.int32, (KV_COLS, 128), 1)
    seg = (head_of_lane == out_lane).astype(BF16)
    lane_t = lax.broadcasted_iota(jnp.int32, (128, KV_COLS), 0)
    head_t = lax.broadcasted_iota(jnp.int32, (128, KV_COLS), 1) // HEAD_DIM
    spread = (lane_t == head_t).astype(BF16)
    lane = lax.broadcasted_iota(jnp.int32, (1, 128), 1)

    for gg in range(GQA):
        qg = q_ref[:, gg * KV_COLS:(gg + 1) * KV_COLS].astype(F32)
        prod = (ck * qg[:, None, :]).astype(BF16).reshape(bb * wk, KV_COLS)
        s = jnp.dot(prod, seg, preferred_element_type=F32).reshape(bb, wk, 128)
        s_new = jnp.dot((kn * qg).astype(BF16), seg, preferred_element_type=F32)
        sink = jnp.zeros((1, 128), F32)
        for hh in range(N_KV):
            sink = jnp.where(lane == hh, sinks_ref[hh * GQA + gg] * LOG2E, sink)
        m = jnp.maximum(jnp.maximum(jnp.max(s, axis=1), s_new), sink)
        p = jnp.exp2(s - m[:, None, :])
        p_new = jnp.exp2(s_new - m)
        denom = jnp.sum(p, axis=1) + p_new + jnp.exp2(sink - m)
        r = 1.0 / denom
        pn = (p * r[:, None, :]).astype(BF16).reshape(bb * wk, 128)
        pn_new = (p_new * r).astype(BF16)
        pe = jnp.dot(pn, spread, preferred_element_type=F32).reshape(bb, wk, KV_COLS)
        pe_new = jnp.dot(pn_new, spread, preferred_element_type=F32)
        o = jnp.sum(pe * cv, axis=1) + pe_new * vn
        attn_ref[:, gg * KV_COLS:(gg + 1) * KV_COLS] = o.astype(BF16)

    nk_ref[:, 0:wk - 1, :] = ck_ref[:, 1:wk, :]
    nk_ref[:, wk - 1:wk, :] = kn[:, None, :]
    nv_ref[:, 0:wk - 1, :] = cv_ref[:, 1:wk, :]
    nv_ref[:, wk - 1:wk, :] = vn[:, None, :]


def _sample_attn(sinks, q, kn, vn, ck, cv, bb):
    bsz, wk = ck.shape[0], ck.shape[1]
    row = lambda i: (i, 0)
    cache = lambda i: (i, 0, 0)
    return pl.pallas_call(
        _sample_attn_kernel,
        grid=(bsz // bb,),
        in_specs=[
            pl.BlockSpec(memory_space=pltpu.SMEM),
            pl.BlockSpec((bb, Q_COLS), row),
            pl.BlockSpec((bb, KV_COLS), row),
            pl.BlockSpec((bb, KV_COLS), row),
            pl.BlockSpec((bb, wk, KV_COLS), cache),
            pl.BlockSpec((bb, wk, KV_COLS), cache),
        ],
        out_specs=[
            pl.BlockSpec((bb, Q_COLS), row),
            pl.BlockSpec((bb, wk, KV_COLS), cache),
            pl.BlockSpec((bb, wk, KV_COLS), cache),
        ],
        out_shape=[
            jax.ShapeDtypeStruct((bsz, Q_COLS), BF16),
            jax.ShapeDtypeStruct((bsz, wk, KV_COLS), F32),
            jax.ShapeDtypeStruct((bsz, wk, KV_COLS), F32),
        ],
        compiler_params=_params(("parallel",)),
        name="sample_attn",
    )(sinks, q, kn, vn, ck, cv)


def _sample_mix_kernel(attn_ref, u_ref, st_ref, h_ref, wpool_ref, pscale_ref, wo_ref, g_ref, b_ref,
                       h1_ref, nst_ref, mix_ref):
    u = u_ref[...]
    mix_ref[:, 0:Q_COLS] = attn_ref[...]
    for g, w in enumerate(POOL_WINDOWS):
        cs = slice(g * POOL_GW, (g + 1) * POOL_GW)
        acc = u[:, cs]
        if w > 1:
            acc = acc + jnp.sum(st_ref[:, POOL_HIST - (w - 1):POOL_HIST, cs], axis=1)
        d = acc * (1.0 / w) - u[:, cs]
        mix_ref[:, Q_COLS + g * POOL_GW:Q_COLS + (g + 1) * POOL_GW] = _pool_group_matmul(d, g, wpool_ref, pscale_ref)
    nst_ref[:, 0:POOL_HIST - 1, :] = st_ref[:, 1:POOL_HIST, :]
    nst_ref[:, POOL_HIST - 1:POOL_HIST, :] = u[:, None, :]
    h1_ref[...] = _outproj_ln(mix_ref[...], h_ref[...], wo_ref, g_ref, b_ref)


def _sample_mix(attn, u, state, h, wpool, pscale, wo, ln_g, ln_b):
    bsz = attn.shape[0]
    return pl.pallas_call(
        _sample_mix_kernel,
        out_shape=[
            jax.ShapeDtypeStruct((bsz, D_MODEL), F32),
            jax.ShapeDtypeStruct((bsz, POOL_HIST, POOL_W), F32),
        ],
        scratch_shapes=[pltpu.VMEM((bsz, D_MODEL), BF16)],
        compiler_params=pltpu.CompilerParams(vmem_limit_bytes=V7X_VMEM_LIMIT),
        name="sample_mix",
    )(attn, u, state, h, wpool, pscale, wo, ln_g, ln_b)


def _ffn_kernel(h_ref, wg_ref, wu_ref, wd_ref, g_ref, b_ref, o_ref, hb_ref):
    f = pl.program_id(1)

    @pl.when(f == 0)
    def _():
        h = h_ref[...]
        hb_ref[...] = h.astype(BF16)
        o_ref[...] = ALPHA * h

    hb = hb_ref[...]
    tf = wg_ref.shape[1]
    part = None
    for c in range(tf // FFN_SUB):
        cs = slice(c * FFN_SUB, (c + 1) * FFN_SUB)
        gate = jnp.dot(hb, wg_ref[:, cs], preferred_element_type=F32)
        up = jnp.dot(hb, wu_ref[:, cs], preferred_element_type=F32)
        a = (gate * (1.0 / (1.0 + jnp.exp(-gate))) * up).astype(BF16)
        p = jnp.dot(a, wd_ref[cs, :], preferred_element_type=F32)
        part = p if part is None else part + p
    o_ref[...] += part

    @pl.when(f == pl.num_programs(1) - 1)
    def _():
        o_ref[...] = _layer_norm(o_ref[...], g_ref[...], b_ref[...])


def _ffn(h1, wg, wu, wd, ln_g, ln_b, tm, tf):
    rows = h1.shape[0]
    return pl.pallas_call(
        _ffn_kernel,
        grid=(rows // tm, D_FF // tf),
        in_specs=[
            pl.BlockSpec((tm, D_MODEL), lambda i, f: (i, 0)),
            pl.BlockSpec((D_MODEL, tf), lambda i, f: (0, f)),
            pl.BlockSpec((D_MODEL, tf), lambda i, f: (0, f)),
            pl.BlockSpec((tf, D_MODEL), lambda i, f: (f, 0)),
            pl.BlockSpec((1, D_MODEL), lambda i, f: (0, 0)),
            pl.BlockSpec((1, D_MODEL), lambda i, f: (0, 0)),
        ],
        out_specs=pl.BlockSpec((tm, D_MODEL), lambda i, f: (i, 0)),
        out_shape=jax.ShapeDtypeStruct((rows, D_MODEL), F32),
        scratch_shapes=[pltpu.VMEM((tm, D_MODEL), BF16)],
        compiler_params=_params(("parallel", "arbitrary")),
        name="ffn",
    )(h1, wg, wu, wd, ln_g, ln_b)


def _rope_tables(pos):
    half = ROT_DIM // 2
    inv_freq = ROPE_THETA ** (-jnp.arange(half, dtype=F32) * 2.0 / ROT_DIM)
    ang = pos.astype(F32)[:, None] * inv_freq
    cos, sin = jnp.cos(ang), jnp.sin(ang)
    t = pos.shape[0]
    ones = jnp.ones((t, HEAD_DIM - ROT_DIM), F32)
    zeros = jnp.zeros((t, HEAD_DIM - ROT_DIM), F32)
    zh = jnp.zeros((t, half), F32)
    c = jnp.concatenate([cos, cos, ones], -1)
    s_lo = jnp.concatenate([-sin, zh, zeros], -1)
    s_hi = jnp.concatenate([zh, sin, zeros], -1)
    return tuple(jnp.concatenate([a, a], -1) for a in (c, s_lo, s_hi))


def kernel(x_prompt, x_sample, cache_k, cache_v, state_pool, meta_tokens, ln_in_g, ln_in_b, w_in, b_in,
           attn_sinks, w_pool, pool_scale, w_o, ln1_g, ln1_b, w_gate, w_up, w_down, ln2_g, ln2_b):
    bp, seq, _ = x_prompt.shape
    bs = x_sample.shape[0]
    wk = cache_k.shape[2]
    assert x_sample.shape[1] == 1 and cache_k.shape[0] == 1

    def q_perm_cols(a):
        lead = a.shape[:-1]
        return a.reshape(lead + (N_KV, GQA, HEAD_DIM)).swapaxes(-3, -2).reshape(lead + (Q_COLS,))

    w_in0, b_in0, w_o0 = w_in[0], b_in[0], w_o[0]
    w_in_p = jnp.concatenate([q_perm_cols(w_in0[:, :Q_COLS]), w_in0[:, Q_COLS:]], -1).astype(BF16)
    b_in_p = jnp.concatenate([q_perm_cols(b_in0[:Q_COLS]), b_in0[Q_COLS:]], -1)[None]
    w_o_attn = w_o0[:Q_COLS].reshape(N_KV, GQA, HEAD_DIM, D_MODEL).swapaxes(0, 1).reshape(Q_COLS, D_MODEL)
    w_o_p = jnp.concatenate([w_o_attn, w_o0[Q_COLS:]], 0).astype(BF16)
    wpool = w_pool[0].astype(BF16)
    pscale = pool_scale[0][None]
    wg, wu, wd = w_gate[0].astype(BF16), w_up[0].astype(BF16), w_down[0].astype(BF16)
    sinks = attn_sinks[0]
    g_in, b_ln_in = ln_in_g[None], ln_in_b[None]
    g1, b1, g2, b2 = ln1_g[0][None], ln1_b[0][None], ln2_g[0][None], ln2_b[0][None]

    tabs_meta = _rope_tables(jnp.arange(N_META, dtype=jnp.int32))
    tabs_prompt = _rope_tables(N_META + jnp.arange(seq, dtype=jnp.int32))
    tabs_sample = _rope_tables(jnp.full((bs,), PAST_LEN, dtype=jnp.int32))

    _, _, k_meta, v_meta, u_meta = _inproj(meta_tokens, g_in, b_ln_in, w_in_p, b_in_p, tabs_meta, N_META)

    tm_in = 512
    h_p, q_p, k_p, v_p, u_p = _inproj(x_prompt.reshape(bp * seq, D_MODEL), g_in, b_ln_in, w_in_p, b_in_p,
                                      tabs_prompt, tm_in)
    k_p3 = k_p.reshape(bp, seq, KV_COLS)
    v_p3 = v_p.reshape(bp, seq, KV_COLS)
    u_p3 = u_p.reshape(bp, seq, POOL_W)
    mix_p = _prompt_mixer(sinks, q_p.reshape(bp, seq, Q_COLS), k_p3, v_p3, u_p3,
                          k_meta, v_meta, u_meta, wpool, pscale, 512)
    h1_p = _outproj(mix_p.reshape(bp * seq, D_MODEL), h_p, w_o_p, g1, b1, 512)
    y_prompt = _ffn(h1_p, wg, wu, wd, g2, b2, 1024, 512).reshape(bp, seq, D_MODEL)

    h_s, q_s, k_s, v_s, u_s = _inproj(x_sample.reshape(bs, D_MODEL), g_in, b_ln_in, w_in_p, b_in_p,
                                      tabs_sample, bs)
    attn_s, nk_s, nv_s = _sample_attn(sinks, q_s, k_s, v_s, cache_k.reshape(bs, wk, KV_COLS),
                                      cache_v.reshape(bs, wk, KV_COLS), 16)
    h1_s, npool_s = _sample_mix(attn_s, u_s, state_pool[0], h_s, wpool, pscale, w_o_p, g1, b1)
    y_sample = _ffn(h1_s, wg, wu, wd, g2, b2, bs, 512).reshape(bs, 1, D_MODEL)

    new_k_prompt = k_p3[:, seq - WINDOW:].reshape(1, bp, WINDOW, N_KV, HEAD_DIM)
    new_v_prompt = v_p3[:, seq - WINDOW:].reshape(1, bp, WINDOW, N_KV, HEAD_DIM)
    new_pool_prompt = u_p3[:, seq - POOL_HIST:][None]
    return (y_prompt, y_sample, new_k_prompt, new_v_prompt, new_pool_prompt,
            nk_s.reshape(1, bs, wk, N_KV, HEAD_DIM), nv_s.reshape(1, bs, wk, N_KV, HEAD_DIM), npool_s[None])
```

```python
import functools
import math

import jax
import jax.numpy as jnp
from jax import lax
from jax.experimental import pallas as pl
from jax.experimental.pallas import tpu as pltpu

D_MODEL = 2048
N_META = 16
HEAD_DIM = 64
N_KV = 4
GQA = 4
Q_COLS = N_KV * GQA * HEAD_DIM
KV_COLS = N_KV * HEAD_DIM
POOL_W = 1024
POOL_WINDOWS = (2, 4, 8, 16)
POOL_GW = POOL_W // len(POOL_WINDOWS)
POOL_HIST = 15
IN_COLS = Q_COLS + 2 * KV_COLS + POOL_W
D_FF = 5632
WINDOW = 128
BLOCK = 128
ROT_DIM = 16
ROPE_THETA = 500000.0
PAST_LEN = 16384
ALPHA = 2.0 ** 0.25
LN_EPS = 1e-5
NEG_INF = -1e30
LOG2E = math.log2(math.e)
QK_SCALE = LOG2E / math.sqrt(HEAD_DIM)
V7X_MXU_COLS = 256
FFN_SUB = V7X_MXU_COLS
ROW_SUB = 256

V7X_VMEM_LIMIT = 62 * 1024 * 1024

BF16 = jnp.bfloat16
F32 = jnp.float32


def _layer_norm(x, g, b):
    mu = jnp.mean(x, axis=-1, keepdims=True)
    xc = x - mu
    var = jnp.mean(xc * xc, axis=-1, keepdims=True)
    return xc * lax.rsqrt(var + LN_EPS) * g + b


def _params(sem):
    return pltpu.CompilerParams(dimension_semantics=sem, vmem_limit_bytes=V7X_VMEM_LIMIT)


def _inproj_kernel(x_ref, g_ref, b_ref, w_ref, bias_ref, c_ref, s1_ref, s2_ref, *rest, n_cast):
    cast_in, (h_ref, q_ref, k_ref, v_ref, u_ref), cast_out = rest[:n_cast], rest[n_cast:n_cast + 5], rest[n_cast + 5:]
    for src_ref, dst_ref in zip(cast_in, cast_out):
        dst_ref[...] = src_ref[...].astype(BF16)
    tm = x_ref.shape[0]
    sub = min(tm, ROW_SUB)
    for r in range(tm // sub):
        rows = slice(r * sub, (r + 1) * sub)
        h = _layer_norm(x_ref[rows, :], g_ref[...], b_ref[...])
        h_ref[rows, :] = h
        z = jnp.dot(h.astype(BF16), w_ref[...], preferred_element_type=F32) + bias_ref[...]
        cos = c_ref[rows, :]
        s_lo = s1_ref[rows, :]
        s_hi = s2_ref[rows, :]

        def rope(zc):
            return (zc * cos + pltpu.roll(zc, 128 - ROT_DIM // 2, 1) * s_lo
                    + pltpu.roll(zc, ROT_DIM // 2, 1) * s_hi)

        for c in range(Q_COLS // 128):
            zc = z[:, c * 128:(c + 1) * 128]
            q_ref[rows, c * 128:(c + 1) * 128] = (rope(zc) * QK_SCALE).astype(BF16)
        for c in range(KV_COLS // 128):
            zc = z[:, Q_COLS + c * 128:Q_COLS + (c + 1) * 128]
            k_ref[rows, c * 128:(c + 1) * 128] = rope(zc)
        v_ref[rows, :] = z[:, Q_COLS + KV_COLS:Q_COLS + 2 * KV_COLS]
        u_ref[rows, :] = z[:, Q_COLS + 2 * KV_COLS:]


def _inproj(x, ln_g, ln_b, w_in, b_in, tabs, tm, cast=()):
    rows = x.shape[0]
    steps = rows // tm
    t_rows = tabs[0].shape[0]
    n_t = t_rows // tm
    row = lambda i: (i, 0)
    fixed = lambda i: (0, 0)
    tab = lambda i: (i % n_t, 0)
    cast_specs = []
    for a in cast:
        span = 1
        while a.shape[0] % (steps // span) or (a.shape[0] // (steps // span)) % 16:
            span *= 2
        cast_specs.append(pl.BlockSpec((a.shape[0] // (steps // span), a.shape[1]),
                                       functools.partial(lambda i, sp: (i // sp, 0), sp=span)))
    return pl.pallas_call(
        functools.partial(_inproj_kernel, n_cast=len(cast)),
        grid=(steps,),
        in_specs=[
            pl.BlockSpec((tm, D_MODEL), row),
            pl.BlockSpec((1, D_MODEL), fixed),
            pl.BlockSpec((1, D_MODEL), fixed),
            pl.BlockSpec((D_MODEL, IN_COLS), fixed),
            pl.BlockSpec((1, IN_COLS), fixed),
            pl.BlockSpec((tm, 128), tab),
            pl.BlockSpec((tm, 128), tab),
            pl.BlockSpec((tm, 128), tab),
        ] + cast_specs,
        out_specs=[
            pl.BlockSpec((tm, D_MODEL), row),
            pl.BlockSpec((tm, Q_COLS), row),
            pl.BlockSpec((tm, KV_COLS), row),
            pl.BlockSpec((tm, KV_COLS), row),
            pl.BlockSpec((tm, POOL_W), row),
        ] + cast_specs,
        out_shape=[
            jax.ShapeDtypeStruct((rows, D_MODEL), F32),
            jax.ShapeDtypeStruct((rows, Q_COLS), BF16),
            jax.ShapeDtypeStruct((rows, KV_COLS), F32),
            jax.ShapeDtypeStruct((rows, KV_COLS), F32),
            jax.ShapeDtypeStruct((rows, POOL_W), F32),
        ] + [jax.ShapeDtypeStruct(a.shape, BF16) for a in cast],
        compiler_params=_params(("arbitrary",)),
        name="inproj",
    )(x, ln_g, ln_b, w_in, b_in, *tabs, *cast)


def _outproj_ln(mix_bf16, h, wo_ref, g_ref, b_ref):
    mix = jnp.dot(mix_bf16, wo_ref[...], preferred_element_type=F32)
    return _layer_norm(ALPHA * h + mix, g_ref[...], b_ref[...])


def _pool_group_matmul(d, g, wpool_ref, pscale_ref):
    cs = slice(g * POOL_GW, (g + 1) * POOL_GW)
    y = jnp.dot(d.astype(BF16), wpool_ref[g], preferred_element_type=F32)
    return (y * pscale_ref[:, cs]).astype(BF16)


def _prompt_mixer_kernel(sinks_ref, q_ref, k_ref, khalo_ref, kmeta_ref, v_ref, vhalo_ref, vmeta_ref,
                         u_ref, uhalo_ref, umeta_ref, wpool_ref, pscale_ref, hprev_ref, wo_ref, g_ref, b_ref,
                         h1_ref, kx_ref, vx_ref, ux_ref, mix_ref, wos_ref, *, tm, n_j, n_tiles):
    t = pl.program_id(0)
    j = jnp.minimum(t, n_tiles - 1) % n_j
    slot = t % 2
    n_blk = tm // BLOCK

    @pl.when(t == 0)
    def _():
        wos_ref[...] = wo_ref[...]
        mix_ref[...] = jnp.zeros(mix_ref.shape, BF16)

    n_col = D_MODEL // V7X_MXU_COLS
    pieces = [(rb, nc) for rb in range(tm // ROW_SUB) for nc in range(n_col)]
    assert len(pieces) == n_blk * N_KV
    parts = []

    def outproj_piece(idx):
        rb, nc = pieces[idx]
        rows = slice(rb * ROW_SUB, (rb + 1) * ROW_SUB)
        cols = slice(nc * V7X_MXU_COLS, (nc + 1) * V7X_MXU_COLS)
        parts.append(jnp.dot(mix_ref[1 - slot, rows, :], wos_ref[:, cols], preferred_element_type=F32))
        if nc == n_col - 1:
            proj = jnp.concatenate(parts, axis=1)
            parts.clear()
            h1_ref[rows, :] = _layer_norm(ALPHA * hprev_ref[rows, :] + proj, g_ref[...], b_ref[...])

    lane_head = lax.broadcasted_iota(jnp.int32, (1, KV_COLS), 1) // HEAD_DIM

    def put_kv(dst_ref, rows, val):
        for hh in range(N_KV):
            dst_ref[hh, rows, :] = jnp.where(lane_head == hh, val, 0.0).astype(BF16)

    pad = BLOCK - N_META

    @pl.when(j == 0)
    def _():
        zeros = jnp.zeros((pad, KV_COLS), F32)
        put_kv(kx_ref, slice(0, pad), zeros)
        put_kv(vx_ref, slice(0, pad), zeros)
        put_kv(kx_ref, slice(pad, BLOCK), kmeta_ref[...])
        put_kv(vx_ref, slice(pad, BLOCK), vmeta_ref[...])
        ux_ref[0:N_META, :] = umeta_ref[...]

    @pl.when(j > 0)
    def _():
        put_kv(kx_ref, slice(0, BLOCK), khalo_ref[...])
        put_kv(vx_ref, slice(0, BLOCK), vhalo_ref[...])
        ux_ref[0:N_META, :] = uhalo_ref[...]

    put_kv(kx_ref, slice(BLOCK, BLOCK + tm), k_ref[...])
    put_kv(vx_ref, slice(BLOCK, BLOCK + tm), v_ref[...])
    ux_ref[N_META:, :] = u_ref[...]

    n_rows = GQA * BLOCK
    qi = lax.broadcasted_iota(jnp.int32, (n_rows, 2 * BLOCK), 0) % BLOCK
    ki = lax.broadcasted_iota(jnp.int32, (n_rows, 2 * BLOCK), 1)
    band = (ki >= qi) & (ki <= qi + WINDOW)
    row_group = lax.broadcasted_iota(jnp.int32, (n_rows, 1), 0) // BLOCK
    sink_cols = []
    for hh in range(N_KV):
        sink = jnp.zeros((n_rows, 1), F32)
        for gg in range(GQA):
            sink = jnp.where(row_group == gg, sinks_ref[hh * GQA + gg] * LOG2E, sink)
        sink_cols.append(sink)

    for blk in range(n_blk):
        r0 = blk * BLOCK
        rows = slice(r0, r0 + BLOCK)

        e = ux_ref[r0:r0 + N_META + BLOCK, :]
        sums = []
        cur = e
        for step in (1, 2, 4, 8):
            cur = cur + pltpu.roll(cur, step, 0)
            sums.append(cur[:, :POOL_GW])
            cur = cur[:, POOL_GW:]
        pooled = []
        for g, w in enumerate(POOL_WINDOWS):
            cs = slice(g * POOL_GW, (g + 1) * POOL_GW)
            d = sums[g][N_META:] * (1.0 / w) - e[N_META:, cs]
            pooled.append(_pool_group_matmul(d, g, wpool_ref, pscale_ref))

        if blk == 0:
            mask = band & (ki >= jnp.where(j == 0, pad, 0))
        else:
            mask = band
        qs = jnp.concatenate([q_ref[rows, gg * KV_COLS:(gg + 1) * KV_COLS] for gg in range(GQA)], axis=0)
        o = jnp.zeros((n_rows, KV_COLS), F32)
        for hh in range(N_KV):
            kb = kx_ref[hh, r0:r0 + 2 * BLOCK, :]
            vb = vx_ref[hh, r0:r0 + 2 * BLOCK, :]
            s = lax.dot_general(qs, kb, (((1,), (1,)), ((), ())), preferred_element_type=F32)
            s = jnp.where(mask, s, NEG_INF)
            sink = sink_cols[hh]
            m = jnp.max(s, axis=-1, keepdims=True)
            p = jnp.exp2(s - m)
            denom = jnp.sum(p, axis=-1, keepdims=True) + jnp.exp2(sink - m)
            pn = (p * (1.0 / denom)).astype(BF16)
            o = o + jnp.dot(pn, vb, preferred_element_type=F32)
            outproj_piece(blk * N_KV + hh)
        attn = [o[gg * BLOCK:(gg + 1) * BLOCK].astype(BF16) for gg in range(GQA)]
        mix_ref[slot, rows, :] = jnp.concatenate(attn + pooled, axis=1)


def _prompt_mixer(sinks, q, k, v, u, h, kmeta, vmeta, umeta, wpool, pscale, wo, ln_g, ln_b, tm):
    bsz, seq = q.shape[0], q.shape[1]
    n_j = seq // tm
    n_tiles = bsz * n_j
    cur = lambda t: jnp.minimum(t, n_tiles - 1)
    tile = lambda t: (cur(t) // n_j, cur(t) % n_j, 0)
    fixed2 = lambda t: (0, 0)
    fixed3 = lambda t: (0, 0, 0)
    kv_halo = lambda t: (cur(t) // n_j, jnp.maximum((cur(t) % n_j) * (tm // BLOCK) - 1, 0), 0)
    u_halo = lambda t: (cur(t) // n_j, jnp.maximum((cur(t) % n_j) * (tm // N_META) - 1, 0), 0)
    prev = lambda t: (jnp.maximum(t - 1, 0), 0)
    return pl.pallas_call(
        functools.partial(_prompt_mixer_kernel, tm=tm, n_j=n_j, n_tiles=n_tiles),
        grid=(n_tiles + 1,),
        in_specs=[
            pl.BlockSpec(memory_space=pltpu.SMEM),
            pl.BlockSpec((None, tm, Q_COLS), tile),
            pl.BlockSpec((None, tm, KV_COLS), tile),
            pl.BlockSpec((None, BLOCK, KV_COLS), kv_halo),
            pl.BlockSpec((N_META, KV_COLS), fixed2),
            pl.BlockSpec((None, tm, KV_COLS), tile),
            pl.BlockSpec((None, BLOCK, KV_COLS), kv_halo),
            pl.BlockSpec((N_META, KV_COLS), fixed2),
            pl.BlockSpec((None, tm, POOL_W), tile),
            pl.BlockSpec((None, N_META, POOL_W), u_halo),
            pl.BlockSpec((N_META, POOL_W), fixed2),
            pl.BlockSpec((len(POOL_WINDOWS), POOL_GW, POOL_GW), fixed3),
            pl.BlockSpec((1, POOL_W), fixed2),
            pl.BlockSpec((tm, D_MODEL), prev),
            pl.BlockSpec((D_MODEL, D_MODEL), fixed2, pipeline_mode=pl.Buffered(1)),
            pl.BlockSpec((1, D_MODEL), fixed2),
            pl.BlockSpec((1, D_MODEL), fixed2),
        ],
        out_specs=pl.BlockSpec((tm, D_MODEL), prev),
        out_shape=jax.ShapeDtypeStruct((bsz * seq, D_MODEL), F32),
        scratch_shapes=[
            pltpu.VMEM((N_KV, BLOCK + tm, KV_COLS), BF16),
            pltpu.VMEM((N_KV, BLOCK + tm, KV_COLS), BF16),
            pltpu.VMEM((N_META + tm, POOL_W), F32),
            pltpu.VMEM((2, tm, D_MODEL), BF16),
            pltpu.VMEM((D_MODEL, D_MODEL), BF16),
        ],
        compiler_params=_params(("arbitrary",)),
        name="prompt_mixer",
    )(sinks, q, k, k, kmeta, v, v, vmeta, u, u, umeta, wpool, pscale, h, wo, ln_g, ln_b)


def _sample_attn_kernel(sinks_ref, q_ref, kn_ref, vn_ref, ck_ref, cv_ref,
                        attn_ref, nk_ref, nv_ref):
    bb = q_ref.shape[0]
    wk = ck_ref.shape[1]
    ck = ck_ref[...]
    cv = cv_ref[...]
    kn = kn_ref[...]
    vn = vn_ref[...]
    head_of_lane = lax.broadcasted_iota(jnp.int32, (KV_COLS, 128), 0) // HEAD_DIM
    out_lane = lax.broadcasted_iota(jnp.int32, (KV_COLS, 128), 1)
    seg = (head_of_lane == out_lane).astype(BF16)
    lane_t = lax.broadcasted_iota(jnp.int32, (128, KV_COLS), 0)
    head_t = lax.broadcasted_iota(jnp.int32, (128, KV_COLS), 1) // HEAD_DIM
    spread = (lane_t == head_t).astype(BF16)
    lane = lax.broadcasted_iota(jnp.int32, (1, 128), 1)

    for gg in range(GQA):
        qg = q_ref[:, gg * KV_COLS:(gg + 1) * KV_COLS].astype(F32)
        prod = (ck * qg[:, None, :]).astype(BF16).reshape(bb * wk, KV_COLS)
        s = jnp.dot(prod, seg, preferred_element_type=F32).reshape(bb, wk, 128)
        s_new = jnp.dot((kn * qg).astype(BF16), seg, preferred_element_type=F32)
        sink = jnp.zeros((1, 128), F32)
        for hh in range(N_KV):
            sink = jnp.where(lane == hh, sinks_ref[hh * GQA + gg] * LOG2E, sink)
        m = jnp.maximum(jnp.maximum(jnp.max(s, axis=1), s_new), sink)
        p = jnp.exp2(s - m[:, None, :])
        p_new = jnp.exp2(s_new - m)
        denom = jnp.sum(p, axis=1) + p_new + jnp.exp2(sink - m)
        r = 1.0 / denom
        pn = (p * r[:, None, :]).astype(BF16).reshape(bb * wk, 128)
        pn_new = (p_new * r).astype(BF16)
        pe = jnp.dot(pn, spread, preferred_element_type=F32).reshape(bb, wk, KV_COLS)
        pe_new = jnp.dot(pn_new, spread, preferred_element_type=F32)
        o = jnp.sum(pe * cv, axis=1) + pe_new * vn
        attn_ref[:, gg * KV_COLS:(gg + 1) * KV_COLS] = o.astype(BF16)

    nk_ref[:, 0:wk - 1, :] = ck_ref[:, 1:wk, :]
    nk_ref[:, wk - 1:wk, :] = kn[:, None, :]
    nv_ref[:, 0:wk - 1, :] = cv_ref[:, 1:wk, :]
    nv_ref[:, wk - 1:wk, :] = vn[:, None, :]


def _sample_attn(sinks, q, kn, vn, ck, cv, bb):
    bsz, wk = ck.shape[0], ck.shape[1]
    row = lambda i: (i, 0)
    cache = lambda i: (i, 0, 0)
    return pl.pallas_call(
        _sample_attn_kernel,
        grid=(bsz // bb,),
        in_specs=[
            pl.BlockSpec(memory_space=pltpu.SMEM),
            pl.BlockSpec((bb, Q_COLS), row),
            pl.BlockSpec((bb, KV_COLS), row),
            pl.BlockSpec((bb, KV_COLS), row),
            pl.BlockSpec((bb, wk, KV_COLS), cache),
            pl.BlockSpec((bb, wk, KV_COLS), cache),
        ],
        out_specs=[
            pl.BlockSpec((bb, Q_COLS), row),
            pl.BlockSpec((bb, wk, KV_COLS), cache),
            pl.BlockSpec((bb, wk, KV_COLS), cache),
        ],
        out_shape=[
            jax.ShapeDtypeStruct((bsz, Q_COLS), BF16),
            jax.ShapeDtypeStruct((bsz, wk, KV_COLS), F32),
            jax.ShapeDtypeStruct((bsz, wk, KV_COLS), F32),
        ],
        compiler_params=_params(("parallel",)),
        name="sample_attn",
    )(sinks, q, kn, vn, ck, cv)


def _sample_mix_kernel(attn_ref, u_ref, st_ref, h_ref, wpool_ref, pscale_ref, wo_ref, g_ref, b_ref,
                       h1_ref, nst_ref, mix_ref):
    u = u_ref[...]
    mix_ref[:, 0:Q_COLS] = attn_ref[...]
    for g, w in enumerate(POOL_WINDOWS):
        cs = slice(g * POOL_GW, (g + 1) * POOL_GW)
        acc = u[:, cs]
        if w > 1:
            acc = acc + jnp.sum(st_ref[:, POOL_HIST - (w - 1):POOL_HIST, cs], axis=1)
        d = acc * (1.0 / w) - u[:, cs]
        mix_ref[:, Q_COLS + g * POOL_GW:Q_COLS + (g + 1) * POOL_GW] = _pool_group_matmul(d, g, wpool_ref, pscale_ref)
    nst_ref[:, 0:POOL_HIST - 1, :] = st_ref[:, 1:POOL_HIST, :]
    nst_ref[:, POOL_HIST - 1:POOL_HIST, :] = u[:, None, :]
    h1_ref[...] = _outproj_ln(mix_ref[...], h_ref[...], wo_ref, g_ref, b_ref)


def _sample_mix(attn, u, state, h, wpool, pscale, wo, ln_g, ln_b):
    bsz = attn.shape[0]
    return pl.pallas_call(
        _sample_mix_kernel,
        out_shape=[
            jax.ShapeDtypeStruct((bsz, D_MODEL), F32),
            jax.ShapeDtypeStruct((bsz, POOL_HIST, POOL_W), F32),
        ],
        scratch_shapes=[pltpu.VMEM((bsz, D_MODEL), BF16)],
        compiler_params=pltpu.CompilerParams(vmem_limit_bytes=V7X_VMEM_LIMIT),
        name="sample_mix",
    )(attn, u, state, h, wpool, pscale, wo, ln_g, ln_b)


def _ffn_kernel(h_ref, wg_ref, wu_ref, wd_ref, g_ref, b_ref, o_ref, hb_ref):
    f = pl.program_id(1)

    @pl.when(f == 0)
    def _():
        h = h_ref[...]
        hb_ref[...] = h.astype(BF16)
        o_ref[...] = ALPHA * h

    hb = hb_ref[...]
    tf = wg_ref.shape[1]
    part = None
    for c in range(tf // FFN_SUB):
        cs = slice(c * FFN_SUB, (c + 1) * FFN_SUB)
        gate = jnp.dot(hb, wg_ref[:, cs], preferred_element_type=F32)
        up = jnp.dot(hb, wu_ref[:, cs], preferred_element_type=F32)
        a = (gate * (1.0 / (1.0 + jnp.exp(-gate))) * up).astype(BF16)
        p = jnp.dot(a, wd_ref[cs, :], preferred_element_type=F32)
        part = p if part is None else part + p
    o_ref[...] += part

    @pl.when(f == pl.num_programs(1) - 1)
    def _():
        o_ref[...] = _layer_norm(o_ref[...], g_ref[...], b_ref[...])


def _ffn(h1, wg, wu, wd, ln_g, ln_b, tm, tf):
    rows = h1.shape[0]
    return pl.pallas_call(
        _ffn_kernel,
        grid=(rows // tm, D_FF // tf),
        in_specs=[
            pl.BlockSpec((tm, D_MODEL), lambda i, f: (i, 0)),
            pl.BlockSpec((D_MODEL, tf), lambda i, f: (0, f)),
            pl.BlockSpec((D_MODEL, tf), lambda i, f: (0, f)),
            pl.BlockSpec((tf, D_MODEL), lambda i, f: (f, 0)),
            pl.BlockSpec((1, D_MODEL), lambda i, f: (0, 0)),
            pl.BlockSpec((1, D_MODEL), lambda i, f: (0, 0)),
        ],
        out_specs=pl.BlockSpec((tm, D_MODEL), lambda i, f: (i, 0)),
        out_shape=jax.ShapeDtypeStruct((rows, D_MODEL), F32),
        scratch_shapes=[pltpu.VMEM((tm, D_MODEL), BF16)],
        compiler_params=_params(("parallel", "arbitrary")),
        name="ffn",
    )(h1, wg, wu, wd, ln_g, ln_b)


def _rope_tables(pos):
    half = ROT_DIM // 2
    inv_freq = ROPE_THETA ** (-jnp.arange(half, dtype=F32) * 2.0 / ROT_DIM)
    ang = pos.astype(F32)[:, None] * inv_freq
    cos, sin = jnp.cos(ang), jnp.sin(ang)
    t = pos.shape[0]
    ones = jnp.ones((t, HEAD_DIM - ROT_DIM), F32)
    zeros = jnp.zeros((t, HEAD_DIM - ROT_DIM), F32)
    zh = jnp.zeros((t, half), F32)
    c = jnp.concatenate([cos, cos, ones], -1)
    s_lo = jnp.concatenate([-sin, zh, zeros], -1)
    s_hi = jnp.concatenate([zh, sin, zeros], -1)
    return tuple(jnp.concatenate([a, a], -1) for a in (c, s_lo, s_hi))


def kernel(x_prompt, x_sample, cache_k, cache_v, state_pool, meta_tokens, ln_in_g, ln_in_b, w_in, b_in,
           attn_sinks, w_pool, pool_scale, w_o, ln1_g, ln1_b, w_gate, w_up, w_down, ln2_g, ln2_b):
    bp, seq, _ = x_prompt.shape
    bs = x_sample.shape[0]
    wk = cache_k.shape[2]
    assert x_sample.shape[1] == 1 and cache_k.shape[0] == 1

    def q_perm_cols(a):
        lead = a.shape[:-1]
        return a.reshape(lead + (N_KV, GQA, HEAD_DIM)).swapaxes(-3, -2).reshape(lead + (Q_COLS,))

    w_in0, b_in0, w_o0 = w_in[0], b_in[0], w_o[0]
    w_in_p = jnp.concatenate([q_perm_cols(w_in0[:, :Q_COLS]), w_in0[:, Q_COLS:]], -1).astype(BF16)
    b_in_p = jnp.concatenate([q_perm_cols(b_in0[:Q_COLS]), b_in0[Q_COLS:]], -1)[None]
    w_o_attn = w_o0[:Q_COLS].reshape(N_KV, GQA, HEAD_DIM, D_MODEL).swapaxes(0, 1).reshape(Q_COLS, D_MODEL)
    w_o_p = jnp.concatenate([w_o_attn, w_o0[Q_COLS:]], 0).astype(BF16)
    wpool = w_pool[0].astype(BF16)
    pscale = pool_scale[0][None]
    sinks = attn_sinks[0]
    g_in, b_ln_in = ln_in_g[None], ln_in_b[None]
    g1, b1, g2, b2 = ln1_g[0][None], ln1_b[0][None], ln2_g[0][None], ln2_b[0][None]

    tabs_meta = _rope_tables(jnp.arange(N_META, dtype=jnp.int32))
    tabs_prompt = _rope_tables(N_META + jnp.arange(seq, dtype=jnp.int32))
    tabs_sample = _rope_tables(jnp.full((bs,), PAST_LEN, dtype=jnp.int32))

    k_meta, v_meta, u_meta = _inproj(meta_tokens, g_in, b_ln_in, w_in_p, b_in_p, tabs_meta, N_META)[2:5]

    tm_in = 512
    h_p, q_p, k_p, v_p, u_p, wg, wu, wd = _inproj(x_prompt.reshape(bp * seq, D_MODEL), g_in, b_ln_in, w_in_p, b_in_p,
                                                  tabs_prompt, tm_in, cast=(w_gate[0], w_up[0], w_down[0]))
    k_p3 = k_p.reshape(bp, seq, KV_COLS)
    v_p3 = v_p.reshape(bp, seq, KV_COLS)
    u_p3 = u_p.reshape(bp, seq, POOL_W)
    h1_p = _prompt_mixer(sinks, q_p.reshape(bp, seq, Q_COLS), k_p3, v_p3, u_p3, h_p,
                         k_meta, v_meta, u_meta, wpool, pscale, w_o_p, g1, b1, 512)
    y_prompt = _ffn(h1_p, wg, wu, wd, g2, b2, 1024, 512).reshape(bp, seq, D_MODEL)

    h_s, q_s, k_s, v_s, u_s = _inproj(x_sample.reshape(bs, D_MODEL), g_in, b_ln_in, w_in_p, b_in_p,
                                      tabs_sample, bs)
    attn_s, nk_s, nv_s = _sample_attn(sinks, q_s, k_s, v_s, cache_k.reshape(bs, wk, KV_COLS),
                                      cache_v.reshape(bs, wk, KV_COLS), 16)
    h1_s, npool_s = _sample_mix(attn_s, u_s, state_pool[0], h_s, wpool, pscale, w_o_p, g1, b1)
    y_sample = _ffn(h1_s, wg, wu, wd, g2, b2, bs, 512).reshape(bs, 1, D_MODEL)

    new_k_prompt = k_p3[:, seq - WINDOW:].reshape(1, bp, WINDOW, N_KV, HEAD_DIM)
    new_v_prompt = v_p3[:, seq - WINDOW:].reshape(1, bp, WINDOW, N_KV, HEAD_DIM)
    new_pool_prompt = u_p3[:, seq - POOL_HIST:][None]
    return (y_prompt, y_sample, new_k_prompt, new_v_prompt, new_pool_prompt,
            nk_s.reshape(1, bs, wk, N_KV, HEAD_DIM), nv_s.reshape(1, bs, wk, N_KV, HEAD_DIM), npool_s[None])
```

```python
import functools
import math

import jax
import jax.numpy as jnp
from jax import lax
from jax.experimental import pallas as pl
from jax.experimental.pallas import tpu as pltpu

D_MODEL = 2048
N_META = 16
HEAD_DIM = 64
N_KV = 4
GQA = 4
Q_COLS = N_KV * GQA * HEAD_DIM
KV_COLS = N_KV * HEAD_DIM
POOL_W = 1024
POOL_WINDOWS = (2, 4, 8, 16)
POOL_GW = POOL_W // len(POOL_WINDOWS)
POOL_HIST = 15
IN_COLS = Q_COLS + 2 * KV_COLS + POOL_W
D_FF = 5632
WINDOW = 128
BLOCK = 128
ROT_DIM = 16
ROPE_THETA = 500000.0
PAST_LEN = 16384
ALPHA = 2.0 ** 0.25
LN_EPS = 1e-5
NEG_INF = -1e30
LOG2E = math.log2(math.e)
QK_SCALE = LOG2E / math.sqrt(HEAD_DIM)
V7X_MXU_COLS = 256
FFN_SUB = V7X_MXU_COLS
ROW_SUB = 256

V7X_VMEM_LIMIT = 62 * 1024 * 1024

BF16 = jnp.bfloat16
F32 = jnp.float32


def _layer_norm(x, g, b):
    mu = jnp.mean(x, axis=-1, keepdims=True)
    xc = x - mu
    var = jnp.mean(xc * xc, axis=-1, keepdims=True)
    return xc * lax.rsqrt(var + LN_EPS) * g + b


def _params(sem):
    return pltpu.CompilerParams(dimension_semantics=sem, vmem_limit_bytes=V7X_VMEM_LIMIT)


def _swap_head_order(cols):
    low = lax.broadcasted_iota(jnp.int32, (1, 128), 1) < HEAD_DIM
    out = []
    for j in range(2 * N_KV):
        b, a0 = j // 2, 2 * (j % 2)
        x = cols[(a0 * 4 + b) // 2]
        y = cols[((a0 + 1) * 4 + b) // 2]
        if b % 2 == 0:
            out.append(jnp.where(low, x, pltpu.roll(y, HEAD_DIM, 1)))
        else:
            out.append(jnp.where(low, pltpu.roll(x, HEAD_DIM, 1), y))
    return out


def _inproj_kernel(x_ref, g_ref, b_ref, w_ref, bias_ref, c_ref, s1_ref, s2_ref, *rest, n_cast):
    cast_in, (h_ref, q_ref, k_ref, v_ref, u_ref), cast_out = rest[:n_cast], rest[n_cast:n_cast + 5], rest[n_cast + 5:]
    for src_ref, dst_ref in zip(cast_in, cast_out):
        dst_ref[...] = src_ref[...].astype(BF16)
    tm = x_ref.shape[0]
    sub = min(tm, ROW_SUB)
    for r in range(tm // sub):
        rows = slice(r * sub, (r + 1) * sub)
        h = _layer_norm(x_ref[rows, :], g_ref[...], b_ref[...])
        h_ref[rows, :] = h
        z = jnp.dot(h.astype(BF16), w_ref[...], preferred_element_type=F32) + bias_ref[...]
        cos = c_ref[rows, :]
        s_lo = s1_ref[rows, :]
        s_hi = s2_ref[rows, :]

        def rope(zc):
            return (zc * cos + pltpu.roll(zc, 128 - ROT_DIM // 2, 1) * s_lo
                    + pltpu.roll(zc, ROT_DIM // 2, 1) * s_hi)

        q_cols = _swap_head_order([rope(z[:, c * 128:(c + 1) * 128]) for c in range(Q_COLS // 128)])
        for c, qc in enumerate(q_cols):
            q_ref[rows, c * 128:(c + 1) * 128] = (qc * QK_SCALE).astype(BF16)
        for c in range(KV_COLS // 128):
            zc = z[:, Q_COLS + c * 128:Q_COLS + (c + 1) * 128]
            k_ref[rows, c * 128:(c + 1) * 128] = rope(zc)
        v_ref[rows, :] = z[:, Q_COLS + KV_COLS:Q_COLS + 2 * KV_COLS]
        u_ref[rows, :] = z[:, Q_COLS + 2 * KV_COLS:]


def _inproj(x, ln_g, ln_b, w_in, b_in, tabs, tm, cast=()):
    rows = x.shape[0]
    steps = rows // tm
    t_rows = tabs[0].shape[0]
    n_t = t_rows // tm
    row = lambda i: (i, 0)
    fixed = lambda i: (0, 0)
    tab = lambda i: (i % n_t, 0)
    cast_specs = []
    for a in cast:
        span = 1
        while a.shape[0] % (steps // span) or (a.shape[0] // (steps // span)) % 16:
            span *= 2
        cast_specs.append(pl.BlockSpec((a.shape[0] // (steps // span), a.shape[1]),
                                       functools.partial(lambda i, sp: (i // sp, 0), sp=span)))
    return pl.pallas_call(
        functools.partial(_inproj_kernel, n_cast=len(cast)),
        grid=(steps,),
        in_specs=[
            pl.BlockSpec((tm, D_MODEL), row),
            pl.BlockSpec((1, D_MODEL), fixed),
            pl.BlockSpec((1, D_MODEL), fixed),
            pl.BlockSpec((D_MODEL, IN_COLS), fixed),
            pl.BlockSpec((1, IN_COLS), fixed),
            pl.BlockSpec((tm, 128), tab),
            pl.BlockSpec((tm, 128), tab),
            pl.BlockSpec((tm, 128), tab),
        ] + cast_specs,
        out_specs=[
            pl.BlockSpec((tm, D_MODEL), row),
            pl.BlockSpec((tm, Q_COLS), row),
            pl.BlockSpec((tm, KV_COLS), row),
            pl.BlockSpec((tm, KV_COLS), row),
            pl.BlockSpec((tm, POOL_W), row),
        ] + cast_specs,
        out_shape=[
            jax.ShapeDtypeStruct((rows, D_MODEL), F32),
            jax.ShapeDtypeStruct((rows, Q_COLS), BF16),
            jax.ShapeDtypeStruct((rows, KV_COLS), F32),
            jax.ShapeDtypeStruct((rows, KV_COLS), F32),
            jax.ShapeDtypeStruct((rows, POOL_W), F32),
        ] + [jax.ShapeDtypeStruct(a.shape, BF16) for a in cast],
        compiler_params=_params(("arbitrary",)),
        name="inproj",
    )(x, ln_g, ln_b, w_in, b_in, *tabs, *cast)


def _outproj_ln(mix_bf16, h, wo_ref, g_ref, b_ref):
    mix = jnp.dot(mix_bf16, wo_ref[...], preferred_element_type=F32)
    return _layer_norm(ALPHA * h + mix, g_ref[...], b_ref[...])


def _pool_group_matmul(d, g, wpool_ref, pscale_ref):
    cs = slice(g * POOL_GW, (g + 1) * POOL_GW)
    y = jnp.dot(d.astype(BF16), wpool_ref[g], preferred_element_type=F32)
    return (y * pscale_ref[:, cs]).astype(BF16)


def _prompt_mixer_kernel(sinks_ref, q_ref, k_ref, khalo_ref, kmeta_ref, v_ref, vhalo_ref, vmeta_ref,
                         u_ref, uhalo_ref, umeta_ref, wpool_ref, pscale_ref, hprev_ref, wo_ref, g_ref, b_ref,
                         h1_ref, kx_ref, vx_ref, ux_ref, mix_ref, wos_ref, *, tm, n_j, n_tiles):
    t = pl.program_id(0)
    j = jnp.minimum(t, n_tiles - 1) % n_j
    slot = t % 2
    n_blk = tm // BLOCK

    @pl.when(t == 0)
    def _():
        wos_ref[...] = wo_ref[...]
        mix_ref[...] = jnp.zeros(mix_ref.shape, BF16)

    n_col = D_MODEL // V7X_MXU_COLS
    pieces = [(rb, nc) for rb in range(tm // ROW_SUB) for nc in range(n_col)]
    assert len(pieces) == n_blk * N_KV
    parts = []

    def outproj_piece(idx):
        rb, nc = pieces[idx]
        rows = slice(rb * ROW_SUB, (rb + 1) * ROW_SUB)
        cols = slice(nc * V7X_MXU_COLS, (nc + 1) * V7X_MXU_COLS)
        parts.append(jnp.dot(mix_ref[1 - slot, rows, :], wos_ref[:, cols], preferred_element_type=F32))
        if nc == n_col - 1:
            proj = jnp.concatenate(parts, axis=1)
            parts.clear()
            h1_ref[rows, :] = _layer_norm(ALPHA * hprev_ref[rows, :] + proj, g_ref[...], b_ref[...])

    lane_head = lax.broadcasted_iota(jnp.int32, (1, KV_COLS), 1) // HEAD_DIM

    def put_kv(dst_ref, rows, val):
        for hh in range(N_KV):
            dst_ref[hh, rows, :] = jnp.where(lane_head == hh, val, 0.0).astype(BF16)

    pad = BLOCK - N_META

    @pl.when(j == 0)
    def _():
        zeros = jnp.zeros((pad, KV_COLS), F32)
        put_kv(kx_ref, slice(0, pad), zeros)
        put_kv(vx_ref, slice(0, pad), zeros)
        put_kv(kx_ref, slice(pad, BLOCK), kmeta_ref[...])
        put_kv(vx_ref, slice(pad, BLOCK), vmeta_ref[...])
        ux_ref[0:N_META, :] = umeta_ref[...]

    @pl.when(j > 0)
    def _():
        put_kv(kx_ref, slice(0, BLOCK), khalo_ref[...])
        put_kv(vx_ref, slice(0, BLOCK), vhalo_ref[...])
        ux_ref[0:N_META, :] = uhalo_ref[...]

    put_kv(kx_ref, slice(BLOCK, BLOCK + tm), k_ref[...])
    put_kv(vx_ref, slice(BLOCK, BLOCK + tm), v_ref[...])
    ux_ref[N_META:, :] = u_ref[...]

    n_rows = GQA * BLOCK
    qi = lax.broadcasted_iota(jnp.int32, (n_rows, 2 * BLOCK), 0) % BLOCK
    ki = lax.broadcasted_iota(jnp.int32, (n_rows, 2 * BLOCK), 1)
    band = (ki >= qi) & (ki <= qi + WINDOW)
    row_group = lax.broadcasted_iota(jnp.int32, (n_rows, 1), 0) // BLOCK
    sink_cols = []
    for hh in range(N_KV):
        sink = jnp.zeros((n_rows, 1), F32)
        for gg in range(GQA):
            sink = jnp.where(row_group == gg, sinks_ref[hh * GQA + gg] * LOG2E, sink)
        sink_cols.append(sink)

    for blk in range(n_blk):
        r0 = blk * BLOCK
        rows = slice(r0, r0 + BLOCK)

        e = ux_ref[r0:r0 + N_META + BLOCK, :]
        sums = []
        cur = e
        for step in (1, 2, 4, 8):
            cur = cur + pltpu.roll(cur, step, 0)
            sums.append(cur[:, :POOL_GW])
            cur = cur[:, POOL_GW:]
        pooled = []
        for g, w in enumerate(POOL_WINDOWS):
            cs = slice(g * POOL_GW, (g + 1) * POOL_GW)
            d = sums[g][N_META:] * (1.0 / w) - e[N_META:, cs]
            pooled.append(_pool_group_matmul(d, g, wpool_ref, pscale_ref))

        if blk == 0:
            mask = band & (ki >= jnp.where(j == 0, pad, 0))
        else:
            mask = band
        qs = jnp.concatenate([q_ref[rows, gg * KV_COLS:(gg + 1) * KV_COLS] for gg in range(GQA)], axis=0)
        o = jnp.zeros((n_rows, KV_COLS), F32)
        for hh in range(N_KV):
            kb = kx_ref[hh, r0:r0 + 2 * BLOCK, :]
            vb = vx_ref[hh, r0:r0 + 2 * BLOCK, :]
            s = lax.dot_general(qs, kb, (((1,), (1,)), ((), ())), preferred_element_type=F32)
            s = jnp.where(mask, s, NEG_INF)
            sink = sink_cols[hh]
            m = jnp.max(s, axis=-1, keepdims=True)
            p = jnp.exp2(s - m)
            denom = jnp.sum(p, axis=-1, keepdims=True) + jnp.exp2(sink - m)
            pn = (p * (1.0 / denom)).astype(BF16)
            o = o + jnp.dot(pn, vb, preferred_element_type=F32)
            outproj_piece(blk * N_KV + hh)
        attn = _swap_head_order([o[gg * BLOCK:(gg + 1) * BLOCK, c * 128:(c + 1) * 128]
                                 for gg in range(GQA) for c in range(KV_COLS // 128)])
        mix_ref[slot, rows, :] = jnp.concatenate([a.astype(BF16) for a in attn] + pooled, axis=1)


def _prompt_mixer(sinks, q, k, v, u, h, kmeta, vmeta, umeta, wpool, pscale, wo, ln_g, ln_b, tm):
    bsz, seq = q.shape[0], q.shape[1]
    n_j = seq // tm
    n_tiles = bsz * n_j
    cur = lambda t: jnp.minimum(t, n_tiles - 1)
    tile = lambda t: (cur(t) // n_j, cur(t) % n_j, 0)
    fixed2 = lambda t: (0, 0)
    fixed3 = lambda t: (0, 0, 0)
    kv_halo = lambda t: (cur(t) // n_j, jnp.maximum((cur(t) % n_j) * (tm // BLOCK) - 1, 0), 0)
    u_halo = lambda t: (cur(t) // n_j, jnp.maximum((cur(t) % n_j) * (tm // N_META) - 1, 0), 0)
    prev = lambda t: (jnp.maximum(t - 1, 0), 0)
    return pl.pallas_call(
        functools.partial(_prompt_mixer_kernel, tm=tm, n_j=n_j, n_tiles=n_tiles),
        grid=(n_tiles + 1,),
        in_specs=[
            pl.BlockSpec(memory_space=pltpu.SMEM),
            pl.BlockSpec((None, tm, Q_COLS), tile),
            pl.BlockSpec((None, tm, KV_COLS), tile),
            pl.BlockSpec((None, BLOCK, KV_COLS), kv_halo),
            pl.BlockSpec((N_META, KV_COLS), fixed2),
            pl.BlockSpec((None, tm, KV_COLS), tile),
            pl.BlockSpec((None, BLOCK, KV_COLS), kv_halo),
            pl.BlockSpec((N_META, KV_COLS), fixed2),
            pl.BlockSpec((None, tm, POOL_W), tile),
            pl.BlockSpec((None, N_META, POOL_W), u_halo),
            pl.BlockSpec((N_META, POOL_W), fixed2),
            pl.BlockSpec((len(POOL_WINDOWS), POOL_GW, POOL_GW), fixed3),
            pl.BlockSpec((1, POOL_W), fixed2),
            pl.BlockSpec((tm, D_MODEL), prev),
            pl.BlockSpec((D_MODEL, D_MODEL), fixed2, pipeline_mode=pl.Buffered(1)),
            pl.BlockSpec((1, D_MODEL), fixed2),
            pl.BlockSpec((1, D_MODEL), fixed2),
        ],
        out_specs=pl.BlockSpec((tm, D_MODEL), prev),
        out_shape=jax.ShapeDtypeStruct((bsz * seq, D_MODEL), F32),
        scratch_shapes=[
            pltpu.VMEM((N_KV, BLOCK + tm, KV_COLS), BF16),
            pltpu.VMEM((N_KV, BLOCK + tm, KV_COLS), BF16),
            pltpu.VMEM((N_META + tm, POOL_W), F32),
            pltpu.VMEM((2, tm, D_MODEL), BF16),
            pltpu.VMEM((D_MODEL, D_MODEL), BF16),
        ],
        compiler_params=_params(("arbitrary",)),
        name="prompt_mixer",
    )(sinks, q, k, k, kmeta, v, v, vmeta, u, u, umeta, wpool, pscale, h, wo, ln_g, ln_b)


def _sample_attn_kernel(sinks_ref, q_ref, kn_ref, vn_ref, ck_ref, cv_ref,
                        attn_ref, nk_ref, nv_ref):
    bb = q_ref.shape[0]
    wk = ck_ref.shape[1]
    ck = ck_ref[...]
    cv = cv_ref[...]
    kn = kn_ref[...]
    vn = vn_ref[...]
    head_of_lane = lax.broadcasted_iota(jnp.int32, (KV_COLS, 128), 0) // HEAD_DIM
    out_lane = lax.broadcasted_iota(jnp.int32, (KV_COLS, 128), 1)
    seg = (head_of_lane == out_lane).astype(BF16)
    lane_t = lax.broadcasted_iota(jnp.int32, (128, KV_COLS), 0)
    head_t = lax.broadcasted_iota(jnp.int32, (128, KV_COLS), 1) // HEAD_DIM
    spread = (lane_t == head_t).astype(BF16)
    lane = lax.broadcasted_iota(jnp.int32, (1, 128), 1)

    o_cols = []
    for gg in range(GQA):
        qg = q_ref[:, gg * KV_COLS:(gg + 1) * KV_COLS].astype(F32)
        prod = (ck * qg[:, None, :]).astype(BF16).reshape(bb * wk, KV_COLS)
        s = jnp.dot(prod, seg, preferred_element_type=F32).reshape(bb, wk, 128)
        s_new = jnp.dot((kn * qg).astype(BF16), seg, preferred_element_type=F32)
        sink = jnp.zeros((1, 128), F32)
        for hh in range(N_KV):
            sink = jnp.where(lane == hh, sinks_ref[hh * GQA + gg] * LOG2E, sink)
        m = jnp.maximum(jnp.maximum(jnp.max(s, axis=1), s_new), sink)
        p = jnp.exp2(s - m[:, None, :])
        p_new = jnp.exp2(s_new - m)
        denom = jnp.sum(p, axis=1) + p_new + jnp.exp2(sink - m)
        r = 1.0 / denom
        pn = (p * r[:, None, :]).astype(BF16).reshape(bb * wk, 128)
        pn_new = (p_new * r).astype(BF16)
        pe = jnp.dot(pn, spread, preferred_element_type=F32).reshape(bb, wk, KV_COLS)
        pe_new = jnp.dot(pn_new, spread, preferred_element_type=F32)
        o = jnp.sum(pe * cv, axis=1) + pe_new * vn
        o_cols += [o[:, c * 128:(c + 1) * 128] for c in range(KV_COLS // 128)]
    for c, oc in enumerate(_swap_head_order(o_cols)):
        attn_ref[:, c * 128:(c + 1) * 128] = oc.astype(BF16)

    nk_ref[:, 0:wk - 1, :] = ck_ref[:, 1:wk, :]
    nk_ref[:, wk - 1:wk, :] = kn[:, None, :]
    nv_ref[:, 0:wk - 1, :] = cv_ref[:, 1:wk, :]
    nv_ref[:, wk - 1:wk, :] = vn[:, None, :]


def _sample_attn(sinks, q, kn, vn, ck, cv, bb):
    bsz, wk = ck.shape[0], ck.shape[1]
    row = lambda i: (i, 0)
    cache = lambda i: (i, 0, 0)
    return pl.pallas_call(
        _sample_attn_kernel,
        grid=(bsz // bb,),
        in_specs=[
            pl.BlockSpec(memory_space=pltpu.SMEM),
            pl.BlockSpec((bb, Q_COLS), row),
            pl.BlockSpec((bb, KV_COLS), row),
            pl.BlockSpec((bb, KV_COLS), row),
            pl.BlockSpec((bb, wk, KV_COLS), cache),
            pl.BlockSpec((bb, wk, KV_COLS), cache),
        ],
        out_specs=[
            pl.BlockSpec((bb, Q_COLS), row),
            pl.BlockSpec((bb, wk, KV_COLS), cache),
            pl.BlockSpec((bb, wk, KV_COLS), cache),
        ],
        out_shape=[
            jax.ShapeDtypeStruct((bsz, Q_COLS), BF16),
            jax.ShapeDtypeStruct((bsz, wk, KV_COLS), F32),
            jax.ShapeDtypeStruct((bsz, wk, KV_COLS), F32),
        ],
        compiler_params=_params(("parallel",)),
        name="sample_attn",
    )(sinks, q, kn, vn, ck, cv)


def _sample_mix_kernel(attn_ref, u_ref, st_ref, h_ref, wpool_ref, pscale_ref, wo_ref, g_ref, b_ref,
                       h1_ref, nst_ref, mix_ref):
    u = u_ref[...]
    mix_ref[:, 0:Q_COLS] = attn_ref[...]
    for g, w in enumerate(POOL_WINDOWS):
        cs = slice(g * POOL_GW, (g + 1) * POOL_GW)
        acc = u[:, cs]
        if w > 1:
            acc = acc + jnp.sum(st_ref[:, POOL_HIST - (w - 1):POOL_HIST, cs], axis=1)
        d = acc * (1.0 / w) - u[:, cs]
        mix_ref[:, Q_COLS + g * POOL_GW:Q_COLS + (g + 1) * POOL_GW] = _pool_group_matmul(d, g, wpool_ref, pscale_ref)
    nst_ref[:, 0:POOL_HIST - 1, :] = st_ref[:, 1:POOL_HIST, :]
    nst_ref[:, POOL_HIST - 1:POOL_HIST, :] = u[:, None, :]
    h1_ref[...] = _outproj_ln(mix_ref[...], h_ref[...], wo_ref, g_ref, b_ref)


def _sample_mix(attn, u, state, h, wpool, pscale, wo, ln_g, ln_b):
    bsz = attn.shape[0]
    return pl.pallas_call(
        _sample_mix_kernel,
        out_shape=[
            jax.ShapeDtypeStruct((bsz, D_MODEL), F32),
            jax.ShapeDtypeStruct((bsz, POOL_HIST, POOL_W), F32),
        ],
        scratch_shapes=[pltpu.VMEM((bsz, D_MODEL), BF16)],
        compiler_params=pltpu.CompilerParams(vmem_limit_bytes=V7X_VMEM_LIMIT),
        name="sample_mix",
    )(attn, u, state, h, wpool, pscale, wo, ln_g, ln_b)


def _ffn_kernel(h_ref, wg_ref, wu_ref, wd_ref, g_ref, b_ref, o_ref):
    f = pl.program_id(1)
    last = pl.num_programs(1) - 1
    tm = h_ref.shape[0]
    tf = wg_ref.shape[1]

    @pl.when(f == 0)
    def _():
        o_ref[...] = ALPHA * h_ref[...]

    def activation(hb, c):
        cs = slice(c * FFN_SUB, (c + 1) * FFN_SUB)
        gate = jnp.dot(hb, wg_ref[:, cs], preferred_element_type=F32)
        up = jnp.dot(hb, wu_ref[:, cs], preferred_element_type=F32)
        return (gate * (1.0 / (1.0 + jnp.exp(-gate))) * up).astype(BF16)

    @pl.when(f < last)
    def _():
        hb = h_ref[...].astype(BF16)
        part = None
        for c in range(tf // FFN_SUB):
            p = jnp.dot(activation(hb, c), wd_ref[c * FFN_SUB:(c + 1) * FFN_SUB, :], preferred_element_type=F32)
            part = p if part is None else part + p
        o_ref[...] += part

    @pl.when(f == last)
    def _():
        hb = h_ref[...].astype(BF16)
        a = jnp.concatenate([activation(hb, c) for c in range(tf // FFN_SUB)], axis=1)
        sub = min(tm, ROW_SUB)
        for r in range(tm // sub):
            rows = slice(r * sub, (r + 1) * sub)
            p = jnp.dot(a[rows], wd_ref[...], preferred_element_type=F32)
            o_ref[rows, :] = _layer_norm(o_ref[rows, :] + p, g_ref[...], b_ref[...])


def _ffn(h1, wg, wu, wd, ln_g, ln_b, tm, tf):
    rows = h1.shape[0]
    return pl.pallas_call(
        _ffn_kernel,
        grid=(rows // tm, D_FF // tf),
        in_specs=[
            pl.BlockSpec((tm, D_MODEL), lambda i, f: (i, 0)),
            pl.BlockSpec((D_MODEL, tf), lambda i, f: (0, f)),
            pl.BlockSpec((D_MODEL, tf), lambda i, f: (0, f)),
            pl.BlockSpec((tf, D_MODEL), lambda i, f: (f, 0)),
            pl.BlockSpec((1, D_MODEL), lambda i, f: (0, 0)),
            pl.BlockSpec((1, D_MODEL), lambda i, f: (0, 0)),
        ],
        out_specs=pl.BlockSpec((tm, D_MODEL), lambda i, f: (i, 0)),
        out_shape=jax.ShapeDtypeStruct((rows, D_MODEL), F32),
        compiler_params=_params(("parallel", "arbitrary")),
        name="ffn",
    )(h1, wg, wu, wd, ln_g, ln_b)


def _rope_tables(pos):
    half = ROT_DIM // 2
    inv_freq = ROPE_THETA ** (-jnp.arange(half, dtype=F32) * 2.0 / ROT_DIM)
    ang = pos.astype(F32)[:, None] * inv_freq
    cos, sin = jnp.cos(ang), jnp.sin(ang)
    t = pos.shape[0]
    ones = jnp.ones((t, HEAD_DIM - ROT_DIM), F32)
    zeros = jnp.zeros((t, HEAD_DIM - ROT_DIM), F32)
    zh = jnp.zeros((t, half), F32)
    c = jnp.concatenate([cos, cos, ones], -1)
    s_lo = jnp.concatenate([-sin, zh, zeros], -1)
    s_hi = jnp.concatenate([zh, sin, zeros], -1)
    return tuple(jnp.concatenate([a, a], -1) for a in (c, s_lo, s_hi))


def kernel(x_prompt, x_sample, cache_k, cache_v, state_pool, meta_tokens, ln_in_g, ln_in_b, w_in, b_in,
           attn_sinks, w_pool, pool_scale, w_o, ln1_g, ln1_b, w_gate, w_up, w_down, ln2_g, ln2_b):
    bp, seq, _ = x_prompt.shape
    bs = x_sample.shape[0]
    wk = cache_k.shape[2]
    assert x_sample.shape[1] == 1 and cache_k.shape[0] == 1

    w_in_p = w_in[0].astype(BF16)
    b_in_p = b_in[0][None]
    w_o_p = w_o[0].astype(BF16)
    wpool = w_pool[0].astype(BF16)
    pscale = pool_scale[0][None]
    sinks = attn_sinks[0]
    g_in, b_ln_in = ln_in_g[None], ln_in_b[None]
    g1, b1, g2, b2 = ln1_g[0][None], ln1_b[0][None], ln2_g[0][None], ln2_b[0][None]

    tabs_meta = _rope_tables(jnp.arange(N_META, dtype=jnp.int32))
    tabs_prompt = _rope_tables(N_META + jnp.arange(seq, dtype=jnp.int32))
    tabs_sample = _rope_tables(jnp.full((bs,), PAST_LEN, dtype=jnp.int32))

    k_meta, v_meta, u_meta = _inproj(meta_tokens, g_in, b_ln_in, w_in_p, b_in_p, tabs_meta, N_META)[2:5]

    tm_in = 512
    h_p, q_p, k_p, v_p, u_p, wg, wu, wd = _inproj(x_prompt.reshape(bp * seq, D_MODEL), g_in, b_ln_in, w_in_p, b_in_p,
                                                  tabs_prompt, tm_in, cast=(w_gate[0], w_up[0], w_down[0]))
    k_p3 = k_p.reshape(bp, seq, KV_COLS)
    v_p3 = v_p.reshape(bp, seq, KV_COLS)
    u_p3 = u_p.reshape(bp, seq, POOL_W)
    h1_p = _prompt_mixer(sinks, q_p.reshape(bp, seq, Q_COLS), k_p3, v_p3, u_p3, h_p,
                         k_meta, v_meta, u_meta, wpool, pscale, w_o_p, g1, b1, 512)
    y_prompt = _ffn(h1_p, wg, wu, wd, g2, b2, 1024, 512).reshape(bp, seq, D_MODEL)

    h_s, q_s, k_s, v_s, u_s = _inproj(x_sample.reshape(bs, D_MODEL), g_in, b_ln_in, w_in_p, b_in_p,
                                      tabs_sample, bs)
    attn_s, nk_s, nv_s = _sample_attn(sinks, q_s, k_s, v_s, cache_k.reshape(bs, wk, KV_COLS),
                                      cache_v.reshape(bs, wk, KV_COLS), 16)
    h1_s, npool_s = _sample_mix(attn_s, u_s, state_pool[0], h_s, wpool, pscale, w_o_p, g1, b1)
    y_sample = _ffn(h1_s, wg, wu, wd, g2, b2, bs, 512).reshape(bs, 1, D_MODEL)

    new_k_prompt = k_p3[:, seq - WINDOW:].reshape(1, bp, WINDOW, N_KV, HEAD_DIM)
    new_v_prompt = v_p3[:, seq - WINDOW:].reshape(1, bp, WINDOW, N_KV, HEAD_DIM)
    new_pool_prompt = u_p3[:, seq - POOL_HIST:][None]
    return (y_prompt, y_sample, new_k_prompt, new_v_prompt, new_pool_prompt,
            nk_s.reshape(1, bs, wk, N_KV, HEAD_DIM), nv_s.reshape(1, bs, wk, N_KV, HEAD_DIM), npool_s[None])
```

```python
import functools
import math

import jax
import jax.numpy as jnp
from jax import lax
from jax.experimental import pallas as pl
from jax.experimental.pallas import tpu as pltpu

D_MODEL = 2048
N_META = 16
HEAD_DIM = 64
N_KV = 4
GQA = 4
Q_COLS = N_KV * GQA * HEAD_DIM
KV_COLS = N_KV * HEAD_DIM
POOL_W = 1024
POOL_WINDOWS = (2, 4, 8, 16)
POOL_GW = POOL_W // len(POOL_WINDOWS)
POOL_HIST = 15
IN_COLS = Q_COLS + 2 * KV_COLS + POOL_W
D_FF = 5632
WINDOW = 128
BLOCK = 128
ROT_DIM = 16
ROPE_THETA = 500000.0
PAST_LEN = 16384
ALPHA = 2.0 ** 0.25
LN_EPS = 1e-5
NEG_INF = -1e30
LOG2E = math.log2(math.e)
QK_SCALE = LOG2E / math.sqrt(HEAD_DIM)
V7X_MXU_COLS = 256
FFN_SUB = V7X_MXU_COLS
ROW_SUB = 256

V7X_VMEM_LIMIT = 62 * 1024 * 1024

BF16 = jnp.bfloat16
F32 = jnp.float32


def _layer_norm(x, g, b):
    mu = jnp.mean(x, axis=-1, keepdims=True)
    xc = x - mu
    var = jnp.mean(xc * xc, axis=-1, keepdims=True)
    return xc * lax.rsqrt(var + LN_EPS) * g + b


def _params(sem):
    return pltpu.CompilerParams(dimension_semantics=sem, vmem_limit_bytes=V7X_VMEM_LIMIT)


def _swap_head_order(cols):
    low = lax.broadcasted_iota(jnp.int32, (1, 128), 1) < HEAD_DIM
    out = []
    for j in range(2 * N_KV):
        b, a0 = j // 2, 2 * (j % 2)
        x = cols[(a0 * 4 + b) // 2]
        y = cols[((a0 + 1) * 4 + b) // 2]
        if b % 2 == 0:
            out.append(jnp.where(low, x, pltpu.roll(y, HEAD_DIM, 1)))
        else:
            out.append(jnp.where(low, pltpu.roll(x, HEAD_DIM, 1), y))
    return out


def _inproj_kernel(x_ref, g_ref, b_ref, w_ref, bias_ref, c_ref, s1_ref, s2_ref, *rest, n_cast):
    cast_in, (h_ref, q_ref, k_ref, v_ref, u_ref), cast_out = rest[:n_cast], rest[n_cast:n_cast + 5], rest[n_cast + 5:]
    for src_ref, dst_ref in zip(cast_in, cast_out):
        dst_ref[...] = src_ref[...].astype(BF16)
    tm = x_ref.shape[0]
    sub = min(tm, ROW_SUB)
    for r in range(tm // sub):
        rows = slice(r * sub, (r + 1) * sub)
        h = _layer_norm(x_ref[rows, :], g_ref[...], b_ref[...])
        h_ref[rows, :] = h
        z = jnp.dot(h.astype(BF16), w_ref[...], preferred_element_type=F32) + bias_ref[...]
        cos = c_ref[rows, :]
        s_lo = s1_ref[rows, :]
        s_hi = s2_ref[rows, :]

        def rope(zc):
            return (zc * cos + pltpu.roll(zc, 128 - ROT_DIM // 2, 1) * s_lo
                    + pltpu.roll(zc, ROT_DIM // 2, 1) * s_hi)

        q_cols = _swap_head_order([rope(z[:, c * 128:(c + 1) * 128]) for c in range(Q_COLS // 128)])
        for c, qc in enumerate(q_cols):
            q_ref[rows, c * 128:(c + 1) * 128] = (qc * QK_SCALE).astype(BF16)
        for c in range(KV_COLS // 128):
            zc = z[:, Q_COLS + c * 128:Q_COLS + (c + 1) * 128]
            k_ref[rows, c * 128:(c + 1) * 128] = rope(zc)
        v_ref[rows, :] = z[:, Q_COLS + KV_COLS:Q_COLS + 2 * KV_COLS]
        u_ref[rows, :] = z[:, Q_COLS + 2 * KV_COLS:]


def _inproj(x, ln_g, ln_b, w_in, b_in, tabs, tm, cast=()):
    rows = x.shape[0]
    steps = rows // tm
    t_rows = tabs[0].shape[0]
    n_t = t_rows // tm
    row = lambda i: (i, 0)
    fixed = lambda i: (0, 0)
    tab = lambda i: (i % n_t, 0)
    cast_specs = []
    for a in cast:
        span = 1
        while a.shape[0] % (steps // span) or (a.shape[0] // (steps // span)) % 16:
            span *= 2
        cast_specs.append(pl.BlockSpec((a.shape[0] // (steps // span), a.shape[1]),
                                       functools.partial(lambda i, sp: (i // sp, 0), sp=span)))
    return pl.pallas_call(
        functools.partial(_inproj_kernel, n_cast=len(cast)),
        grid=(steps,),
        in_specs=[
            pl.BlockSpec((tm, D_MODEL), row),
            pl.BlockSpec((1, D_MODEL), fixed),
            pl.BlockSpec((1, D_MODEL), fixed),
            pl.BlockSpec((D_MODEL, IN_COLS), fixed),
            pl.BlockSpec((1, IN_COLS), fixed),
            pl.BlockSpec((tm, 128), tab),
            pl.BlockSpec((tm, 128), tab),
            pl.BlockSpec((tm, 128), tab),
        ] + cast_specs,
        out_specs=[
            pl.BlockSpec((tm, D_MODEL), row),
            pl.BlockSpec((tm, Q_COLS), row),
            pl.BlockSpec((tm, KV_COLS), row),
            pl.BlockSpec((tm, KV_COLS), row),
            pl.BlockSpec((tm, POOL_W), row),
        ] + cast_specs,
        out_shape=[
            jax.ShapeDtypeStruct((rows, D_MODEL), F32),
            jax.ShapeDtypeStruct((rows, Q_COLS), BF16),
            jax.ShapeDtypeStruct((rows, KV_COLS), F32),
            jax.ShapeDtypeStruct((rows, KV_COLS), F32),
            jax.ShapeDtypeStruct((rows, POOL_W), F32),
        ] + [jax.ShapeDtypeStruct(a.shape, BF16) for a in cast],
        compiler_params=_params(("arbitrary",)),
        name="inproj",
    )(x, ln_g, ln_b, w_in, b_in, *tabs, *cast)


def _outproj_ln(mix_bf16, h, wo_ref, g_ref, b_ref):
    mix = jnp.dot(mix_bf16, wo_ref[...], preferred_element_type=F32)
    return _layer_norm(ALPHA * h + mix, g_ref[...], b_ref[...])


def _pool_group_matmul(d, g, wpool_ref, pscale_ref):
    cs = slice(g * POOL_GW, (g + 1) * POOL_GW)
    y = jnp.dot(d.astype(BF16), wpool_ref[g], preferred_element_type=F32)
    return (y * pscale_ref[:, cs]).astype(BF16)


def _prompt_mixer_kernel(sinks_ref, q_ref, k_ref, khalo_ref, kmeta_ref, v_ref, vhalo_ref, vmeta_ref,
                         u_ref, uhalo_ref, umeta_ref, wpool_ref, pscale_ref, hprev_ref, wo_ref, g_ref, b_ref,
                         h1_ref, kx_ref, vx_ref, ux_ref, mix_ref, wos_ref, *, tm, n_j, n_tiles):
    t = pl.program_id(0)
    j = jnp.minimum(t, n_tiles - 1) % n_j
    slot = t % 2
    n_blk = tm // BLOCK

    @pl.when(t == 0)
    def _():
        wos_ref[...] = wo_ref[...]
        mix_ref[...] = jnp.zeros(mix_ref.shape, BF16)

    n_col = D_MODEL // V7X_MXU_COLS
    pieces = [(rb, nc) for rb in range(tm // ROW_SUB) for nc in range(n_col)]
    assert len(pieces) == n_blk * N_KV
    parts = []

    def outproj_piece(idx):
        rb, nc = pieces[idx]
        rows = slice(rb * ROW_SUB, (rb + 1) * ROW_SUB)
        cols = slice(nc * V7X_MXU_COLS, (nc + 1) * V7X_MXU_COLS)
        parts.append(jnp.dot(mix_ref[1 - slot, rows, :], wos_ref[:, cols], preferred_element_type=F32))
        if nc == n_col - 1:
            proj = jnp.concatenate(parts, axis=1)
            parts.clear()
            h1_ref[rows, :] = _layer_norm(ALPHA * hprev_ref[rows, :] + proj, g_ref[...], b_ref[...])

    lane_head = lax.broadcasted_iota(jnp.int32, (1, KV_COLS), 1) // HEAD_DIM

    def put_kv(dst_ref, rows, val):
        for hh in range(N_KV):
            dst_ref[hh, rows, :] = jnp.where(lane_head == hh, val, 0.0).astype(BF16)

    pad = BLOCK - N_META

    @pl.when(j == 0)
    def _():
        zeros = jnp.zeros((pad, KV_COLS), F32)
        put_kv(kx_ref, slice(0, pad), zeros)
        put_kv(vx_ref, slice(0, pad), zeros)
        put_kv(kx_ref, slice(pad, BLOCK), kmeta_ref[...])
        put_kv(vx_ref, slice(pad, BLOCK), vmeta_ref[...])
        ux_ref[0:N_META, :] = umeta_ref[...]

    @pl.when(j > 0)
    def _():
        put_kv(kx_ref, slice(0, BLOCK), khalo_ref[...])
        put_kv(vx_ref, slice(0, BLOCK), vhalo_ref[...])
        ux_ref[0:N_META, :] = uhalo_ref[...]

    put_kv(kx_ref, slice(BLOCK, BLOCK + tm), k_ref[...])
    put_kv(vx_ref, slice(BLOCK, BLOCK + tm), v_ref[...])
    ux_ref[N_META:, :] = u_ref[...]

    n_rows = GQA * BLOCK
    qi = lax.broadcasted_iota(jnp.int32, (n_rows, 2 * BLOCK), 0) % BLOCK
    ki = lax.broadcasted_iota(jnp.int32, (n_rows, 2 * BLOCK), 1)
    band = (ki >= qi) & (ki <= qi + WINDOW)
    row_group = lax.broadcasted_iota(jnp.int32, (n_rows, 1), 0) // BLOCK
    sink_cols = []
    for hh in range(N_KV):
        sink = jnp.zeros((n_rows, 1), F32)
        for gg in range(GQA):
            sink = jnp.where(row_group == gg, sinks_ref[hh * GQA + gg] * LOG2E, sink)
        sink_cols.append(sink)

    for blk in range(n_blk):
        r0 = blk * BLOCK
        rows = slice(r0, r0 + BLOCK)

        e = ux_ref[r0:r0 + N_META + BLOCK, :]
        sums = []
        cur = e
        for step in (1, 2, 4, 8):
            cur = cur + pltpu.roll(cur, step, 0)
            sums.append(cur[:, :POOL_GW])
            cur = cur[:, POOL_GW:]
        pooled = []
        for g, w in enumerate(POOL_WINDOWS):
            cs = slice(g * POOL_GW, (g + 1) * POOL_GW)
            d = sums[g][N_META:] * (1.0 / w) - e[N_META:, cs]
            pooled.append(_pool_group_matmul(d, g, wpool_ref, pscale_ref))

        if blk == 0:
            mask = band & (ki >= jnp.where(j == 0, pad, 0))
        else:
            mask = band
        qs = jnp.concatenate([q_ref[rows, gg * KV_COLS:(gg + 1) * KV_COLS] for gg in range(GQA)], axis=0)
        o = jnp.zeros((n_rows, KV_COLS), F32)
        for hh in range(N_KV):
            kb = kx_ref[hh, r0:r0 + 2 * BLOCK, :]
            vb = vx_ref[hh, r0:r0 + 2 * BLOCK, :]
            s = lax.dot_general(qs, kb, (((1,), (1,)), ((), ())), preferred_element_type=F32)
            s = jnp.where(mask, s, NEG_INF)
            sink = sink_cols[hh]
            m = jnp.max(s, axis=-1, keepdims=True)
            p = jnp.exp2(s - m)
            denom = jnp.sum(p, axis=-1, keepdims=True) + jnp.exp2(sink - m)
            pn = (p * (1.0 / denom)).astype(BF16)
            o = o + jnp.dot(pn, vb, preferred_element_type=F32)
            outproj_piece(blk * N_KV + hh)
        attn = _swap_head_order([o[gg * BLOCK:(gg + 1) * BLOCK, c * 128:(c + 1) * 128]
                                 for gg in range(GQA) for c in range(KV_COLS // 128)])
        mix_ref[slot, rows, :] = jnp.concatenate([a.astype(BF16) for a in attn] + pooled, axis=1)


def _prompt_mixer(sinks, q, k, v, u, h, kmeta, vmeta, umeta, wpool, pscale, wo, ln_g, ln_b, tm):
    bsz, seq = q.shape[0], q.shape[1]
    n_j = seq // tm
    n_tiles = bsz * n_j
    cur = lambda t: jnp.minimum(t, n_tiles - 1)
    tile = lambda t: (cur(t) // n_j, cur(t) % n_j, 0)
    fixed2 = lambda t: (0, 0)
    fixed3 = lambda t: (0, 0, 0)
    kv_halo = lambda t: (cur(t) // n_j, jnp.maximum((cur(t) % n_j) * (tm // BLOCK) - 1, 0), 0)
    u_halo = lambda t: (cur(t) // n_j, jnp.maximum((cur(t) % n_j) * (tm // N_META) - 1, 0), 0)
    prev = lambda t: (jnp.maximum(t - 1, 0), 0)
    return pl.pallas_call(
        functools.partial(_prompt_mixer_kernel, tm=tm, n_j=n_j, n_tiles=n_tiles),
        grid=(n_tiles + 1,),
        in_specs=[
            pl.BlockSpec(memory_space=pltpu.SMEM),
            pl.BlockSpec((None, tm, Q_COLS), tile),
            pl.BlockSpec((None, tm, KV_COLS), tile),
            pl.BlockSpec((None, BLOCK, KV_COLS), kv_halo),
            pl.BlockSpec((N_META, KV_COLS), fixed2),
            pl.BlockSpec((None, tm, KV_COLS), tile),
            pl.BlockSpec((None, BLOCK, KV_COLS), kv_halo),
            pl.BlockSpec((N_META, KV_COLS), fixed2),
            pl.BlockSpec((None, tm, POOL_W), tile),
            pl.BlockSpec((None, N_META, POOL_W), u_halo),
            pl.BlockSpec((N_META, POOL_W), fixed2),
            pl.BlockSpec((len(POOL_WINDOWS), POOL_GW, POOL_GW), fixed3),
            pl.BlockSpec((1, POOL_W), fixed2),
            pl.BlockSpec((tm, D_MODEL), prev),
            pl.BlockSpec((D_MODEL, D_MODEL), fixed2, pipeline_mode=pl.Buffered(1)),
            pl.BlockSpec((1, D_MODEL), fixed2),
            pl.BlockSpec((1, D_MODEL), fixed2),
        ],
        out_specs=pl.BlockSpec((tm, D_MODEL), prev),
        out_shape=jax.ShapeDtypeStruct((bsz * seq, D_MODEL), F32),
        scratch_shapes=[
            pltpu.VMEM((N_KV, BLOCK + tm, KV_COLS), BF16),
            pltpu.VMEM((N_KV, BLOCK + tm, KV_COLS), BF16),
            pltpu.VMEM((N_META + tm, POOL_W), F32),
            pltpu.VMEM((2, tm, D_MODEL), BF16),
            pltpu.VMEM((D_MODEL, D_MODEL), BF16),
        ],
        compiler_params=_params(("arbitrary",)),
        name="prompt_mixer",
    )(sinks, q, k, k, kmeta, v, v, vmeta, u, u, umeta, wpool, pscale, h, wo, ln_g, ln_b)


def _sample_attn_kernel(sinks_ref, q_ref, kn_ref, vn_ref, ck_ref, cv_ref,
                        attn_ref, nk_ref, nv_ref):
    bb = q_ref.shape[0]
    wk = ck_ref.shape[1]
    ck = ck_ref[...]
    cv = cv_ref[...]
    kn = kn_ref[...]
    vn = vn_ref[...]
    head_of_lane = lax.broadcasted_iota(jnp.int32, (KV_COLS, 128), 0) // HEAD_DIM
    out_lane = lax.broadcasted_iota(jnp.int32, (KV_COLS, 128), 1)
    seg = (head_of_lane == out_lane).astype(BF16)
    lane_t = lax.broadcasted_iota(jnp.int32, (128, KV_COLS), 0)
    head_t = lax.broadcasted_iota(jnp.int32, (128, KV_COLS), 1) // HEAD_DIM
    spread = (lane_t == head_t).astype(BF16)
    lane = lax.broadcasted_iota(jnp.int32, (1, 128), 1)

    o_cols = []
    for gg in range(GQA):
        qg = q_ref[:, gg * KV_COLS:(gg + 1) * KV_COLS].astype(F32)
        prod = (ck * qg[:, None, :]).astype(BF16).reshape(bb * wk, KV_COLS)
        s = jnp.dot(prod, seg, preferred_element_type=F32).reshape(bb, wk, 128)
        s_new = jnp.dot((kn * qg).astype(BF16), seg, preferred_element_type=F32)
        sink = jnp.zeros((1, 128), F32)
        for hh in range(N_KV):
            sink = jnp.where(lane == hh, sinks_ref[hh * GQA + gg] * LOG2E, sink)
        m = jnp.maximum(jnp.maximum(jnp.max(s, axis=1), s_new), sink)
        p = jnp.exp2(s - m[:, None, :])
        p_new = jnp.exp2(s_new - m)
        denom = jnp.sum(p, axis=1) + p_new + jnp.exp2(sink - m)
        r = 1.0 / denom
        pn = (p * r[:, None, :]).astype(BF16).reshape(bb * wk, 128)
        pn_new = (p_new * r).astype(BF16)
        pe = jnp.dot(pn, spread, preferred_element_type=F32).reshape(bb, wk, KV_COLS)
        pe_new = jnp.dot(pn_new, spread, preferred_element_type=F32)
        o = jnp.sum(pe * cv, axis=1) + pe_new * vn
        o_cols += [o[:, c * 128:(c + 1) * 128] for c in range(KV_COLS // 128)]
    for c, oc in enumerate(_swap_head_order(o_cols)):
        attn_ref[:, c * 128:(c + 1) * 128] = oc.astype(BF16)

    nk_ref[:, 0:wk - 1, :] = ck_ref[:, 1:wk, :]
    nk_ref[:, wk - 1:wk, :] = kn[:, None, :]
    nv_ref[:, 0:wk - 1, :] = cv_ref[:, 1:wk, :]
    nv_ref[:, wk - 1:wk, :] = vn[:, None, :]


def _sample_attn(sinks, q, kn, vn, ck, cv, bb):
    bsz, wk = ck.shape[0], ck.shape[1]
    row = lambda i: (i, 0)
    cache = lambda i: (i, 0, 0)
    return pl.pallas_call(
        _sample_attn_kernel,
        grid=(bsz // bb,),
        in_specs=[
            pl.BlockSpec(memory_space=pltpu.SMEM),
            pl.BlockSpec((bb, Q_COLS), row),
            pl.BlockSpec((bb, KV_COLS), row),
            pl.BlockSpec((bb, KV_COLS), row),
            pl.BlockSpec((bb, wk, KV_COLS), cache),
            pl.BlockSpec((bb, wk, KV_COLS), cache),
        ],
        out_specs=[
            pl.BlockSpec((bb, Q_COLS), row),
            pl.BlockSpec((bb, wk, KV_COLS), cache),
            pl.BlockSpec((bb, wk, KV_COLS), cache),
        ],
        out_shape=[
            jax.ShapeDtypeStruct((bsz, Q_COLS), BF16),
            jax.ShapeDtypeStruct((bsz, wk, KV_COLS), F32),
            jax.ShapeDtypeStruct((bsz, wk, KV_COLS), F32),
        ],
        compiler_params=_params(("parallel",)),
        name="sample_attn",
    )(sinks, q, kn, vn, ck, cv)


def _sample_mix_kernel(attn_ref, u_ref, st_ref, h_ref, wpool_ref, pscale_ref, wo_ref, g_ref, b_ref,
                       h1_ref, nst_ref, mix_ref):
    u = u_ref[...]
    mix_ref[:, 0:Q_COLS] = attn_ref[...]
    for g, w in enumerate(POOL_WINDOWS):
        cs = slice(g * POOL_GW, (g + 1) * POOL_GW)
        acc = u[:, cs]
        for t in range(POOL_HIST - (w - 1), POOL_HIST):
            acc = acc + st_ref[t, :, cs]
        d = acc * (1.0 / w) - u[:, cs]
        mix_ref[:, Q_COLS + g * POOL_GW:Q_COLS + (g + 1) * POOL_GW] = _pool_group_matmul(d, g, wpool_ref, pscale_ref)
    nst_ref[0:POOL_HIST - 1] = st_ref[1:POOL_HIST]
    nst_ref[POOL_HIST - 1] = u
    h1_ref[...] = _outproj_ln(mix_ref[...], h_ref[...], wo_ref, g_ref, b_ref)


def _sample_mix(attn, u, state, h, wpool, pscale, wo, ln_g, ln_b):
    bsz = attn.shape[0]
    return pl.pallas_call(
        _sample_mix_kernel,
        out_shape=[
            jax.ShapeDtypeStruct((bsz, D_MODEL), F32),
            jax.ShapeDtypeStruct((POOL_HIST, bsz, POOL_W), F32),
        ],
        scratch_shapes=[pltpu.VMEM((bsz, D_MODEL), BF16)],
        compiler_params=pltpu.CompilerParams(vmem_limit_bytes=V7X_VMEM_LIMIT),
        name="sample_mix",
    )(attn, u, state, h, wpool, pscale, wo, ln_g, ln_b)


def _ffn_kernel(h_ref, wg_ref, wu_ref, wd_ref, g_ref, b_ref, o_ref):
    f = pl.program_id(1)
    last = pl.num_programs(1) - 1
    tm = h_ref.shape[0]
    tf = wg_ref.shape[1]

    @pl.when(f == 0)
    def _():
        o_ref[...] = ALPHA * h_ref[...]

    def activation(hb, c):
        cs = slice(c * FFN_SUB, (c + 1) * FFN_SUB)
        gate = jnp.dot(hb, wg_ref[:, cs], preferred_element_type=F32)
        up = jnp.dot(hb, wu_ref[:, cs], preferred_element_type=F32)
        return (gate * (1.0 / (1.0 + jnp.exp(-gate))) * up).astype(BF16)

    @pl.when(f < last)
    def _():
        hb = h_ref[...].astype(BF16)
        part = None
        for c in range(tf // FFN_SUB):
            p = jnp.dot(activation(hb, c), wd_ref[c * FFN_SUB:(c + 1) * FFN_SUB, :], preferred_element_type=F32)
            part = p if part is None else part + p
        o_ref[...] += part

    @pl.when(f == last)
    def _():
        hb = h_ref[...].astype(BF16)
        a = jnp.concatenate([activation(hb, c) for c in range(tf // FFN_SUB)], axis=1)
        sub = min(tm, ROW_SUB)
        for r in range(tm // sub):
            rows = slice(r * sub, (r + 1) * sub)
            p = jnp.dot(a[rows], wd_ref[...], preferred_element_type=F32)
            o_ref[rows, :] = _layer_norm(o_ref[rows, :] + p, g_ref[...], b_ref[...])


def _ffn(h1, wg, wu, wd, ln_g, ln_b, tm, tf):
    rows = h1.shape[0]
    return pl.pallas_call(
        _ffn_kernel,
        grid=(rows // tm, D_FF // tf),
        in_specs=[
            pl.BlockSpec((tm, D_MODEL), lambda i, f: (i, 0)),
            pl.BlockSpec((D_MODEL, tf), lambda i, f: (0, f)),
            pl.BlockSpec((D_MODEL, tf), lambda i, f: (0, f)),
            pl.BlockSpec((tf, D_MODEL), lambda i, f: (f, 0)),
            pl.BlockSpec((1, D_MODEL), lambda i, f: (0, 0)),
            pl.BlockSpec((1, D_MODEL), lambda i, f: (0, 0)),
        ],
        out_specs=pl.BlockSpec((tm, D_MODEL), lambda i, f: (i, 0)),
        out_shape=jax.ShapeDtypeStruct((rows, D_MODEL), F32),
        compiler_params=_params(("parallel", "arbitrary")),
        name="ffn",
    )(h1, wg, wu, wd, ln_g, ln_b)


def _rope_tables(pos):
    half = ROT_DIM // 2
    inv_freq = ROPE_THETA ** (-jnp.arange(half, dtype=F32) * 2.0 / ROT_DIM)
    ang = pos.astype(F32)[:, None] * inv_freq
    cos, sin = jnp.cos(ang), jnp.sin(ang)
    t = pos.shape[0]
    ones = jnp.ones((t, HEAD_DIM - ROT_DIM), F32)
    zeros = jnp.zeros((t, HEAD_DIM - ROT_DIM), F32)
    zh = jnp.zeros((t, half), F32)
    c = jnp.concatenate([cos, cos, ones], -1)
    s_lo = jnp.concatenate([-sin, zh, zeros], -1)
    s_hi = jnp.concatenate([zh, sin, zeros], -1)
    return tuple(jnp.concatenate([a, a], -1) for a in (c, s_lo, s_hi))


def kernel(x_prompt, x_sample, cache_k, cache_v, state_pool, meta_tokens, ln_in_g, ln_in_b, w_in, b_in,
           attn_sinks, w_pool, pool_scale, w_o, ln1_g, ln1_b, w_gate, w_up, w_down, ln2_g, ln2_b):
    bp, seq, _ = x_prompt.shape
    bs = x_sample.shape[0]
    wk = cache_k.shape[2]
    assert x_sample.shape[1] == 1 and cache_k.shape[0] == 1

    w_in_p = w_in[0].astype(BF16)
    b_in_p = b_in[0][None]
    wpool = w_pool[0].astype(BF16)
    pscale = pool_scale[0][None]
    sinks = attn_sinks[0]
    g_in, b_ln_in = ln_in_g[None], ln_in_b[None]
    g1, b1, g2, b2 = ln1_g[0][None], ln1_b[0][None], ln2_g[0][None], ln2_b[0][None]

    tabs_prompt = _rope_tables(N_META + jnp.arange(seq, dtype=jnp.int32))
    pos_small = jnp.concatenate([jnp.arange(N_META, dtype=jnp.int32), jnp.full((bs,), PAST_LEN, dtype=jnp.int32)])
    tabs_small = _rope_tables(pos_small)

    x_small = jnp.concatenate([meta_tokens, x_sample.reshape(bs, D_MODEL)], 0)
    h_ms, q_ms, k_ms, v_ms, u_ms = _inproj(x_small, g_in, b_ln_in, w_in_p, b_in_p, tabs_small, N_META + bs)
    k_meta, v_meta, u_meta = k_ms[:N_META], v_ms[:N_META], u_ms[:N_META]
    h_s, q_s, k_s, v_s, u_s = h_ms[N_META:], q_ms[N_META:], k_ms[N_META:], v_ms[N_META:], u_ms[N_META:]

    tm_in = 512
    h_p, q_p, k_p, v_p, u_p, wg, wu, wd, w_o_p = _inproj(
        x_prompt.reshape(bp * seq, D_MODEL), g_in, b_ln_in, w_in_p, b_in_p, tabs_prompt, tm_in,
        cast=(w_gate[0], w_up[0], w_down[0], w_o[0]))
    k_p3 = k_p.reshape(bp, seq, KV_COLS)
    v_p3 = v_p.reshape(bp, seq, KV_COLS)
    u_p3 = u_p.reshape(bp, seq, POOL_W)
    h1_p = _prompt_mixer(sinks, q_p.reshape(bp, seq, Q_COLS), k_p3, v_p3, u_p3, h_p,
                         k_meta, v_meta, u_meta, wpool, pscale, w_o_p, g1, b1, 512)
    y_prompt = _ffn(h1_p, wg, wu, wd, g2, b2, 1024, 512).reshape(bp, seq, D_MODEL)

    attn_s, nk_s, nv_s = _sample_attn(sinks, q_s, k_s, v_s, cache_k.reshape(bs, wk, KV_COLS),
                                      cache_v.reshape(bs, wk, KV_COLS), 16)
    h1_s, npool_s = _sample_mix(attn_s, u_s, state_pool[0].swapaxes(0, 1), h_s, wpool, pscale, w_o_p, g1, b1)
    y_sample = _ffn(h1_s, wg, wu, wd, g2, b2, bs, 512).reshape(bs, 1, D_MODEL)

    new_k_prompt = k_p3[:, seq - WINDOW:].reshape(1, bp, WINDOW, N_KV, HEAD_DIM)
    new_v_prompt = v_p3[:, seq - WINDOW:].reshape(1, bp, WINDOW, N_KV, HEAD_DIM)
    new_pool_prompt = u_p3[:, seq - POOL_HIST:][None]
    return (y_prompt, y_sample, new_k_prompt, new_v_prompt, new_pool_prompt,
            nk_s.reshape(1, bs, wk, N_KV, HEAD_DIM), nv_s.reshape(1, bs, wk, N_KV, HEAD_DIM), npool_s.swapaxes(0, 1)[None])
```

```python
import functools
import math

import jax
import numpy as np
import jax.numpy as jnp
from jax import lax
from jax.experimental import pallas as pl
from jax.experimental.pallas import tpu as pltpu

D_MODEL = 2048
N_META = 16
HEAD_DIM = 64
N_KV = 4
GQA = 4
Q_COLS = N_KV * GQA * HEAD_DIM
KV_COLS = N_KV * HEAD_DIM
POOL_W = 1024
POOL_WINDOWS = (2, 4, 8, 16)
POOL_GW = POOL_W // len(POOL_WINDOWS)
POOL_HIST = 15
IN_COLS = Q_COLS + 2 * KV_COLS + POOL_W
D_FF = 5632
WINDOW = 128
BLOCK = 128
ROT_DIM = 16
ROPE_THETA = 500000.0
PAST_LEN = 16384
ALPHA = 2.0 ** 0.25
LN_EPS = 1e-5
NEG_INF = -1e30
LOG2E = math.log2(math.e)
QK_SCALE = LOG2E / math.sqrt(HEAD_DIM)
V7X_MXU_COLS = 256
FFN_SUB = V7X_MXU_COLS
ROW_SUB = 256

V7X_VMEM_LIMIT = 62 * 1024 * 1024

BF16 = jnp.bfloat16
F32 = jnp.float32


def _layer_norm(x, g, b):
    mu = jnp.mean(x, axis=-1, keepdims=True)
    xc = x - mu
    var = jnp.mean(xc * xc, axis=-1, keepdims=True)
    return xc * lax.rsqrt(var + LN_EPS) * g + b


def _params(sem):
    return pltpu.CompilerParams(dimension_semantics=sem, vmem_limit_bytes=V7X_VMEM_LIMIT)


def _swap_head_order(cols):
    low = lax.broadcasted_iota(jnp.int32, (1, 128), 1) < HEAD_DIM
    out = []
    for j in range(2 * N_KV):
        b, a0 = j // 2, 2 * (j % 2)
        x = cols[(a0 * 4 + b) // 2]
        y = cols[((a0 + 1) * 4 + b) // 2]
        if b % 2 == 0:
            out.append(jnp.where(low, x, pltpu.roll(y, HEAD_DIM, 1)))
        else:
            out.append(jnp.where(low, pltpu.roll(x, HEAD_DIM, 1), y))
    return out


def _inproj_kernel(x_ref, g_ref, b_ref, w_ref, bias_ref, c_ref, s1_ref, s2_ref, *rest, n_cast):
    cast_in, (h_ref, q_ref, k_ref, v_ref, u_ref) = rest[:n_cast], rest[n_cast:n_cast + 5]
    cast_out, tails = rest[n_cast + 5:2 * n_cast + 5], rest[2 * n_cast + 5:]
    for src_ref, dst_ref in zip(cast_in, cast_out):
        dst_ref[...] = src_ref[...].astype(BF16)
    tm = x_ref.shape[0]
    sub = min(tm, ROW_SUB)
    for r in range(tm // sub):
        rows = slice(r * sub, (r + 1) * sub)
        h = _layer_norm(x_ref[rows, :], g_ref[...], b_ref[...])
        h_ref[rows, :] = h
        z = jnp.dot(h.astype(BF16), w_ref[...], preferred_element_type=F32) + bias_ref[...]
        cos = c_ref[rows, :]
        s_lo = s1_ref[rows, :]
        s_hi = s2_ref[rows, :]

        def rope(zc):
            return (zc * cos + pltpu.roll(zc, 128 - ROT_DIM // 2, 1) * s_lo
                    + pltpu.roll(zc, ROT_DIM // 2, 1) * s_hi)

        q_cols = _swap_head_order([rope(z[:, c * 128:(c + 1) * 128]) for c in range(Q_COLS // 128)])
        for c, qc in enumerate(q_cols):
            q_ref[rows, c * 128:(c + 1) * 128] = (qc * QK_SCALE).astype(BF16)
        k = jnp.concatenate([rope(z[:, Q_COLS + c * 128:Q_COLS + (c + 1) * 128]) for c in range(KV_COLS // 128)], 1)
        v = z[:, Q_COLS + KV_COLS:Q_COLS + 2 * KV_COLS]
        u = z[:, Q_COLS + 2 * KV_COLS:]
        k_ref[rows, :] = k.astype(k_ref.dtype)
        v_ref[rows, :] = v.astype(v_ref.dtype)
        u_ref[rows, :] = u
        if tails and r == tm // sub - 1:
            k_tail, v_tail, u_tail = tails
            k_tail[...] = k[sub - WINDOW:]
            v_tail[...] = v[sub - WINDOW:]
            u_tail[...] = u[sub - N_META:]


def _inproj(x, ln_g, ln_b, w_in, b_in, tabs, tm, cast=(), seq_steps=None):
    rows = x.shape[0]
    steps = rows // tm
    kv_dtype = F32 if seq_steps is None else BF16
    tail_specs, tail_shapes = [], []
    if seq_steps is not None:
        n_seq = steps // seq_steps
        seq_of = lambda i: (i // seq_steps, 0, 0)
        for n_rows, width in ((WINDOW, KV_COLS), (WINDOW, KV_COLS), (N_META, POOL_W)):
            tail_specs.append(pl.BlockSpec((None, n_rows, width), seq_of))
            tail_shapes.append(jax.ShapeDtypeStruct((n_seq, n_rows, width), F32))
    t_rows = tabs[0].shape[0]
    n_t = t_rows // tm
    row = lambda i: (i, 0)
    fixed = lambda i: (0, 0)
    tab = lambda i: (i % n_t, 0)
    cast_specs = []
    for a in cast:
        span = 1
        while a.shape[0] % (steps // span) or (a.shape[0] // (steps // span)) % 16:
            span *= 2
        cast_specs.append(pl.BlockSpec((a.shape[0] // (steps // span), a.shape[1]),
                                       functools.partial(lambda i, sp: (i // sp, 0), sp=span)))
    return pl.pallas_call(
        functools.partial(_inproj_kernel, n_cast=len(cast)),
        grid=(steps,),
        in_specs=[
            pl.BlockSpec((tm, D_MODEL), row),
            pl.BlockSpec((1, D_MODEL), fixed),
            pl.BlockSpec((1, D_MODEL), fixed),
            pl.BlockSpec((D_MODEL, IN_COLS), fixed),
            pl.BlockSpec((1, IN_COLS), fixed),
            pl.BlockSpec((tm, 128), tab),
            pl.BlockSpec((tm, 128), tab),
            pl.BlockSpec((tm, 128), tab),
        ] + cast_specs,
        out_specs=[
            pl.BlockSpec((tm, D_MODEL), row),
            pl.BlockSpec((tm, Q_COLS), row),
            pl.BlockSpec((tm, KV_COLS), row),
            pl.BlockSpec((tm, KV_COLS), row),
            pl.BlockSpec((tm, POOL_W), row),
        ] + cast_specs + tail_specs,
        out_shape=[
            jax.ShapeDtypeStruct((rows, D_MODEL), F32),
            jax.ShapeDtypeStruct((rows, Q_COLS), BF16),
            jax.ShapeDtypeStruct((rows, KV_COLS), kv_dtype),
            jax.ShapeDtypeStruct((rows, KV_COLS), kv_dtype),
            jax.ShapeDtypeStruct((rows, POOL_W), F32),
        ] + [jax.ShapeDtypeStruct(a.shape, BF16) for a in cast] + tail_shapes,
        compiler_params=_params(("arbitrary",)),
        name="inproj",
    )(x, ln_g, ln_b, w_in, b_in, *tabs, *cast)


def _outproj_ln(mix_bf16, h, wo_ref, g_ref, b_ref):
    mix = jnp.dot(mix_bf16, wo_ref[...], preferred_element_type=F32)
    return _layer_norm(ALPHA * h + mix, g_ref[...], b_ref[...])


def _pool_group_matmul(d, g, wpool_ref, pscale_ref):
    cs = slice(g * POOL_GW, (g + 1) * POOL_GW)
    y = jnp.dot(d.astype(BF16), wpool_ref[g], preferred_element_type=F32)
    return (y * pscale_ref[:, cs]).astype(BF16)


def _prompt_mixer_kernel(sinks_ref, q_ref, k_ref, khalo_ref, kmeta_ref, v_ref, vhalo_ref, vmeta_ref,
                         u_ref, uhalo_ref, umeta_ref, wpool_ref, pscale_ref, hprev_ref, wo_ref, g_ref, b_ref,
                         h1_ref, kx_ref, vx_ref, ux_ref, mix_ref, wos_ref, *, tm, n_j, n_tiles):
    t = pl.program_id(0)
    j = jnp.minimum(t, n_tiles - 1) % n_j
    slot = t % 2
    n_blk = tm // BLOCK

    @pl.when(t == 0)
    def _():
        wos_ref[...] = wo_ref[...]
        mix_ref[...] = jnp.zeros(mix_ref.shape, BF16)

    n_col = D_MODEL // V7X_MXU_COLS
    pieces = [(rb, nc) for rb in range(tm // ROW_SUB) for nc in range(n_col)]
    assert len(pieces) == n_blk * N_KV
    parts = []

    def outproj_piece(idx):
        rb, nc = pieces[idx]
        rows = slice(rb * ROW_SUB, (rb + 1) * ROW_SUB)
        cols = slice(nc * V7X_MXU_COLS, (nc + 1) * V7X_MXU_COLS)
        parts.append(jnp.dot(mix_ref[1 - slot, rows, :], wos_ref[:, cols], preferred_element_type=F32))
        if nc == n_col - 1:
            proj = jnp.concatenate(parts, axis=1)
            parts.clear()
            h1_ref[rows, :] = _layer_norm(ALPHA * hprev_ref[rows, :] + proj, g_ref[...], b_ref[...])

    lane_head = lax.broadcasted_iota(jnp.int32, (1, KV_COLS), 1) // HEAD_DIM
    pad = BLOCK - N_META

    @pl.when(j == 0)
    def _():
        zeros = jnp.zeros((pad, KV_COLS), BF16)
        kx_ref[0:pad, :] = zeros
        vx_ref[0:pad, :] = zeros
        kx_ref[pad:BLOCK, :] = kmeta_ref[...].astype(BF16)
        vx_ref[pad:BLOCK, :] = vmeta_ref[...].astype(BF16)
        ux_ref[0:N_META, :] = umeta_ref[...]

    @pl.when(j > 0)
    def _():
        kx_ref[0:BLOCK, :] = khalo_ref[...]
        vx_ref[0:BLOCK, :] = vhalo_ref[...]
        ux_ref[0:N_META, :] = uhalo_ref[...]

    kx_ref[BLOCK:, :] = k_ref[...]
    vx_ref[BLOCK:, :] = v_ref[...]
    ux_ref[N_META:, :] = u_ref[...]

    n_rows = GQA * BLOCK
    qi = lax.broadcasted_iota(jnp.int32, (n_rows, 2 * BLOCK), 0) % BLOCK
    ki = lax.broadcasted_iota(jnp.int32, (n_rows, 2 * BLOCK), 1)
    band = (ki >= qi) & (ki <= qi + WINDOW)
    row_group = lax.broadcasted_iota(jnp.int32, (n_rows, 1), 0) // BLOCK
    sink_cols = []
    for hh in range(N_KV):
        sink = jnp.zeros((n_rows, 1), F32)
        for gg in range(GQA):
            sink = jnp.where(row_group == gg, sinks_ref[hh * GQA + gg] * LOG2E, sink)
        sink_cols.append(sink)

    for blk in range(n_blk):
        r0 = blk * BLOCK
        rows = slice(r0, r0 + BLOCK)

        e = ux_ref[r0:r0 + N_META + BLOCK, :]
        sums = []
        cur = e
        for step in (1, 2, 4, 8):
            cur = cur + pltpu.roll(cur, step, 0)
            sums.append(cur[:, :POOL_GW])
            cur = cur[:, POOL_GW:]
        pooled = []
        for g, w in enumerate(POOL_WINDOWS):
            cs = slice(g * POOL_GW, (g + 1) * POOL_GW)
            d = sums[g][N_META:] * (1.0 / w) - e[N_META:, cs]
            pooled.append(_pool_group_matmul(d, g, wpool_ref, pscale_ref))

        if blk == 0:
            mask = band & (ki >= jnp.where(j == 0, pad, 0))
        else:
            mask = band
        qs = jnp.concatenate([q_ref[rows, gg * KV_COLS:(gg + 1) * KV_COLS] for gg in range(GQA)], axis=0)
        o = jnp.zeros((n_rows, KV_COLS), F32)
        for hh in range(N_KV):
            kb = jnp.where(lane_head == hh, kx_ref[r0:r0 + 2 * BLOCK, :], 0)
            vb = jnp.where(lane_head == hh, vx_ref[r0:r0 + 2 * BLOCK, :], 0)
            s = lax.dot_general(qs, kb, (((1,), (1,)), ((), ())), preferred_element_type=F32)
            s = jnp.where(mask, s, NEG_INF)
            sink = sink_cols[hh]
            m = jnp.max(s, axis=-1, keepdims=True)
            p = jnp.exp2(s - m)
            denom = jnp.sum(p, axis=-1, keepdims=True) + jnp.exp2(sink - m)
            pn = (p * (1.0 / denom)).astype(BF16)
            o = o + jnp.dot(pn, vb, preferred_element_type=F32)
            outproj_piece(blk * N_KV + hh)
        attn = _swap_head_order([o[gg * BLOCK:(gg + 1) * BLOCK, c * 128:(c + 1) * 128]
                                 for gg in range(GQA) for c in range(KV_COLS // 128)])
        mix_ref[slot, rows, :] = jnp.concatenate([a.astype(BF16) for a in attn] + pooled, axis=1)


def _prompt_mixer(sinks, q, k, v, u, h, kmeta, vmeta, umeta, wpool, pscale, wo, ln_g, ln_b, tm):
    bsz, seq = q.shape[0], q.shape[1]
    n_j = seq // tm
    n_tiles = bsz * n_j
    cur = lambda t: jnp.minimum(t, n_tiles - 1)
    tile = lambda t: (cur(t) // n_j, cur(t) % n_j, 0)
    fixed2 = lambda t: (0, 0)
    fixed3 = lambda t: (0, 0, 0)
    kv_halo = lambda t: (cur(t) // n_j, jnp.maximum((cur(t) % n_j) * (tm // BLOCK) - 1, 0), 0)
    u_halo = lambda t: (cur(t) // n_j, jnp.maximum((cur(t) % n_j) * (tm // N_META) - 1, 0), 0)
    prev = lambda t: (jnp.maximum(t - 1, 0), 0)
    return pl.pallas_call(
        functools.partial(_prompt_mixer_kernel, tm=tm, n_j=n_j, n_tiles=n_tiles),
        grid=(n_tiles + 1,),
        in_specs=[
            pl.BlockSpec(memory_space=pltpu.SMEM),
            pl.BlockSpec((None, tm, Q_COLS), tile),
            pl.BlockSpec((None, tm, KV_COLS), tile),
            pl.BlockSpec((None, BLOCK, KV_COLS), kv_halo),
            pl.BlockSpec((N_META, KV_COLS), fixed2),
            pl.BlockSpec((None, tm, KV_COLS), tile),
            pl.BlockSpec((None, BLOCK, KV_COLS), kv_halo),
            pl.BlockSpec((N_META, KV_COLS), fixed2),
            pl.BlockSpec((None, tm, POOL_W), tile),
            pl.BlockSpec((None, N_META, POOL_W), u_halo),
            pl.BlockSpec((N_META, POOL_W), fixed2),
            pl.BlockSpec((len(POOL_WINDOWS), POOL_GW, POOL_GW), fixed3),
            pl.BlockSpec((1, POOL_W), fixed2),
            pl.BlockSpec((tm, D_MODEL), prev),
            pl.BlockSpec((D_MODEL, D_MODEL), fixed2, pipeline_mode=pl.Buffered(1)),
            pl.BlockSpec((1, D_MODEL), fixed2),
            pl.BlockSpec((1, D_MODEL), fixed2),
        ],
        out_specs=pl.BlockSpec((tm, D_MODEL), prev),
        out_shape=jax.ShapeDtypeStruct((bsz * seq, D_MODEL), F32),
        scratch_shapes=[
            pltpu.VMEM((BLOCK + tm, KV_COLS), BF16),
            pltpu.VMEM((BLOCK + tm, KV_COLS), BF16),
            pltpu.VMEM((N_META + tm, POOL_W), F32),
            pltpu.VMEM((2, tm, D_MODEL), BF16),
            pltpu.VMEM((D_MODEL, D_MODEL), BF16),
        ],
        compiler_params=_params(("arbitrary",)),
        name="prompt_mixer",
    )(sinks, q, k, k, kmeta, v, v, vmeta, u, u, umeta, wpool, pscale, h, wo, ln_g, ln_b)


def _sample_attn_kernel(sinks_ref, q_ref, kn_ref, vn_ref, ck_ref, cv_ref,
                        attn_ref, nk_ref, nv_ref):
    bb = q_ref.shape[0]
    wk = ck_ref.shape[1]
    ck = ck_ref[...]
    cv = cv_ref[...]
    kn = kn_ref[...]
    vn = vn_ref[...]
    head_of_lane = lax.broadcasted_iota(jnp.int32, (KV_COLS, 128), 0) // HEAD_DIM
    out_lane = lax.broadcasted_iota(jnp.int32, (KV_COLS, 128), 1)
    seg = (head_of_lane == out_lane).astype(BF16)
    lane_t = lax.broadcasted_iota(jnp.int32, (128, KV_COLS), 0)
    head_t = lax.broadcasted_iota(jnp.int32, (128, KV_COLS), 1) // HEAD_DIM
    spread = (lane_t == head_t).astype(BF16)
    lane = lax.broadcasted_iota(jnp.int32, (1, 128), 1)

    o_cols = []
    for gg in range(GQA):
        qg = q_ref[:, gg * KV_COLS:(gg + 1) * KV_COLS].astype(F32)
        prod = (ck * qg[:, None, :]).astype(BF16).reshape(bb * wk, KV_COLS)
        s = jnp.dot(prod, seg, preferred_element_type=F32).reshape(bb, wk, 128)
        s_new = jnp.dot((kn * qg).astype(BF16), seg, preferred_element_type=F32)
        sink = jnp.zeros((1, 128), F32)
        for hh in range(N_KV):
            sink = jnp.where(lane == hh, sinks_ref[hh * GQA + gg] * LOG2E, sink)
        m = jnp.maximum(jnp.maximum(jnp.max(s, axis=1), s_new), sink)
        p = jnp.exp2(s - m[:, None, :])
        p_new = jnp.exp2(s_new - m)
        denom = jnp.sum(p, axis=1) + p_new + jnp.exp2(sink - m)
        r = 1.0 / denom
        pn = (p * r[:, None, :]).astype(BF16).reshape(bb * wk, 128)
        pn_new = (p_new * r).astype(BF16)
        pe = jnp.dot(pn, spread, preferred_element_type=F32).reshape(bb, wk, KV_COLS)
        pe_new = jnp.dot(pn_new, spread, preferred_element_type=F32)
        o = jnp.sum(pe * cv, axis=1) + pe_new * vn
        o_cols += [o[:, c * 128:(c + 1) * 128] for c in range(KV_COLS // 128)]
    for c, oc in enumerate(_swap_head_order(o_cols)):
        attn_ref[:, c * 128:(c + 1) * 128] = oc.astype(BF16)

    nk_ref[:, 0:wk - 1, :] = ck_ref[:, 1:wk, :]
    nk_ref[:, wk - 1:wk, :] = kn[:, None, :]
    nv_ref[:, 0:wk - 1, :] = cv_ref[:, 1:wk, :]
    nv_ref[:, wk - 1:wk, :] = vn[:, None, :]


def _sample_attn(sinks, q, kn, vn, ck, cv, bb):
    bsz, wk = ck.shape[0], ck.shape[1]
    row = lambda i: (i, 0)
    cache = lambda i: (i, 0, 0)
    return pl.pallas_call(
        _sample_attn_kernel,
        grid=(bsz // bb,),
        in_specs=[
            pl.BlockSpec(memory_space=pltpu.SMEM),
            pl.BlockSpec((bb, Q_COLS), row),
            pl.BlockSpec((bb, KV_COLS), row),
            pl.BlockSpec((bb, KV_COLS), row),
            pl.BlockSpec((bb, wk, KV_COLS), cache),
            pl.BlockSpec((bb, wk, KV_COLS), cache),
        ],
        out_specs=[
            pl.BlockSpec((bb, Q_COLS), row),
            pl.BlockSpec((bb, wk, KV_COLS), cache),
            pl.BlockSpec((bb, wk, KV_COLS), cache),
        ],
        out_shape=[
            jax.ShapeDtypeStruct((bsz, Q_COLS), BF16),
            jax.ShapeDtypeStruct((bsz, wk, KV_COLS), F32),
            jax.ShapeDtypeStruct((bsz, wk, KV_COLS), F32),
        ],
        compiler_params=_params(("parallel",)),
        name="sample_attn",
    )(sinks, q, kn, vn, ck, cv)


def _sample_mix_kernel(attn_ref, u_ref, st_ref, h_ref, wpool_ref, pscale_ref, wo_ref, g_ref, b_ref,
                       h1_ref, nst_ref, mix_ref):
    u = u_ref[...]
    mix_ref[:, 0:Q_COLS] = attn_ref[...]
    for g, w in enumerate(POOL_WINDOWS):
        cs = slice(g * POOL_GW, (g + 1) * POOL_GW)
        acc = u[:, cs]
        for t in range(POOL_HIST - (w - 1), POOL_HIST):
            acc = acc + st_ref[t, :, cs]
        d = acc * (1.0 / w) - u[:, cs]
        mix_ref[:, Q_COLS + g * POOL_GW:Q_COLS + (g + 1) * POOL_GW] = _pool_group_matmul(d, g, wpool_ref, pscale_ref)
    nst_ref[0:POOL_HIST - 1] = st_ref[1:POOL_HIST]
    nst_ref[POOL_HIST - 1] = u
    h1_ref[...] = _outproj_ln(mix_ref[...], h_ref[...], wo_ref, g_ref, b_ref)


def _sample_mix(attn, u, state, h, wpool, pscale, wo, ln_g, ln_b):
    bsz = attn.shape[0]
    return pl.pallas_call(
        _sample_mix_kernel,
        out_shape=[
            jax.ShapeDtypeStruct((bsz, D_MODEL), F32),
            jax.ShapeDtypeStruct((POOL_HIST, bsz, POOL_W), F32),
        ],
        scratch_shapes=[pltpu.VMEM((bsz, D_MODEL), BF16)],
        compiler_params=pltpu.CompilerParams(vmem_limit_bytes=V7X_VMEM_LIMIT),
        name="sample_mix",
    )(attn, u, state, h, wpool, pscale, wo, ln_g, ln_b)


def _ffn_kernel(h_ref, wg_ref, wu_ref, wd_ref, g_ref, b_ref, o_ref):
    f = pl.program_id(1)
    last = pl.num_programs(1) - 1
    tm = h_ref.shape[0]
    tf = wg_ref.shape[1]

    @pl.when(f == 0)
    def _():
        o_ref[...] = ALPHA * h_ref[...]

    def activation(hb, c):
        cs = slice(c * FFN_SUB, (c + 1) * FFN_SUB)
        gate = jnp.dot(hb, wg_ref[:, cs], preferred_element_type=F32)
        up = jnp.dot(hb, wu_ref[:, cs], preferred_element_type=F32)
        return (gate * (1.0 / (1.0 + jnp.exp(-gate))) * up).astype(BF16)

    @pl.when(f < last)
    def _():
        hb = h_ref[...].astype(BF16)
        part = None
        for c in range(tf // FFN_SUB):
            p = jnp.dot(activation(hb, c), wd_ref[c * FFN_SUB:(c + 1) * FFN_SUB, :], preferred_element_type=F32)
            part = p if part is None else part + p
        o_ref[...] += part

    @pl.when(f == last)
    def _():
        hb = h_ref[...].astype(BF16)
        a = jnp.concatenate([activation(hb, c) for c in range(tf // FFN_SUB)], axis=1)
        sub = min(tm, ROW_SUB)
        for r in range(tm // sub):
            rows = slice(r * sub, (r + 1) * sub)
            p = jnp.dot(a[rows], wd_ref[...], preferred_element_type=F32)
            o_ref[rows, :] = _layer_norm(o_ref[rows, :] + p, g_ref[...], b_ref[...])


def _ffn(h1, wg, wu, wd, ln_g, ln_b, tm, tf):
    rows = h1.shape[0]
    return pl.pallas_call(
        _ffn_kernel,
        grid=(rows // tm, D_FF // tf),
        in_specs=[
            pl.BlockSpec((tm, D_MODEL), lambda i, f: (i, 0)),
            pl.BlockSpec((D_MODEL, tf), lambda i, f: (0, f)),
            pl.BlockSpec((D_MODEL, tf), lambda i, f: (0, f)),
            pl.BlockSpec((tf, D_MODEL), lambda i, f: (f, 0)),
            pl.BlockSpec((1, D_MODEL), lambda i, f: (0, 0)),
            pl.BlockSpec((1, D_MODEL), lambda i, f: (0, 0)),
        ],
        out_specs=pl.BlockSpec((tm, D_MODEL), lambda i, f: (i, 0)),
        out_shape=jax.ShapeDtypeStruct((rows, D_MODEL), F32),
        compiler_params=_params(("parallel", "arbitrary")),
        name="ffn",
    )(h1, wg, wu, wd, ln_g, ln_b)


def _rope_tables(pos):
    half = ROT_DIM // 2
    inv_freq = ROPE_THETA ** (-np.arange(half, dtype=np.float64) * 2.0 / ROT_DIM)
    ang = np.asarray(pos, np.float64)[:, None] * inv_freq
    cos, sin = np.cos(ang), np.sin(ang)
    t = ang.shape[0]
    ones = np.ones((t, HEAD_DIM - ROT_DIM))
    zeros = np.zeros((t, HEAD_DIM - ROT_DIM))
    zh = np.zeros((t, half))
    c = np.concatenate([cos, cos, ones], -1)
    s_lo = np.concatenate([-sin, zh, zeros], -1)
    s_hi = np.concatenate([zh, sin, zeros], -1)
    return tuple(jnp.asarray(np.concatenate([a, a], -1), F32) for a in (c, s_lo, s_hi))


def kernel(x_prompt, x_sample, cache_k, cache_v, state_pool, meta_tokens, ln_in_g, ln_in_b, w_in, b_in,
           attn_sinks, w_pool, pool_scale, w_o, ln1_g, ln1_b, w_gate, w_up, w_down, ln2_g, ln2_b):
    bp, seq, _ = x_prompt.shape
    bs = x_sample.shape[0]
    wk = cache_k.shape[2]
    assert x_sample.shape[1] == 1 and cache_k.shape[0] == 1

    w_in_p = w_in[0].astype(BF16)
    b_in_p = b_in[0][None]
    wpool = w_pool[0].astype(BF16)
    pscale = pool_scale[0][None]
    sinks = attn_sinks[0]
    g_in, b_ln_in = ln_in_g[None], ln_in_b[None]
    g1, b1, g2, b2 = ln1_g[0][None], ln1_b[0][None], ln2_g[0][None], ln2_b[0][None]

    tabs_prompt = _rope_tables(N_META + np.arange(seq))
    tabs_small = _rope_tables(np.concatenate([np.arange(N_META), np.full((bs,), PAST_LEN)]))

    x_small = jnp.concatenate([meta_tokens, x_sample.reshape(bs, D_MODEL)], 0)
    h_ms, q_ms, k_ms, v_ms, u_ms = _inproj(x_small, g_in, b_ln_in, w_in_p, b_in_p, tabs_small, N_META + bs)
    k_meta, v_meta, u_meta = k_ms[:N_META], v_ms[:N_META], u_ms[:N_META]
    h_s, q_s, k_s, v_s, u_s = h_ms[N_META:], q_ms[N_META:], k_ms[N_META:], v_ms[N_META:], u_ms[N_META:]

    tm_in = 512
    h_p, q_p, k_p, v_p, u_p, wg, wu, wd, w_o_p, k_tail, v_tail, u_tail = _inproj(
        x_prompt.reshape(bp * seq, D_MODEL), g_in, b_ln_in, w_in_p, b_in_p, tabs_prompt, tm_in,
        cast=(w_gate[0], w_up[0], w_down[0], w_o[0]), seq_steps=seq // tm_in)
    k_p3 = k_p.reshape(bp, seq, KV_COLS)
    v_p3 = v_p.reshape(bp, seq, KV_COLS)
    u_p3 = u_p.reshape(bp, seq, POOL_W)
    h1_p = _prompt_mixer(sinks, q_p.reshape(bp, seq, Q_COLS), k_p3, v_p3, u_p3, h_p,
                         k_meta, v_meta, u_meta, wpool, pscale, w_o_p, g1, b1, 512)
    y_prompt = _ffn(h1_p, wg, wu, wd, g2, b2, 1024, 512).reshape(bp, seq, D_MODEL)

    attn_s, nk_s, nv_s = _sample_attn(sinks, q_s, k_s, v_s, cache_k.reshape(bs, wk, KV_COLS),
                                      cache_v.reshape(bs, wk, KV_COLS), 16)
    h1_s, npool_s = _sample_mix(attn_s, u_s, state_pool[0].swapaxes(0, 1), h_s, wpool, pscale, w_o_p, g1, b1)
    y_sample = _ffn(h1_s, wg, wu, wd, g2, b2, bs, 512).reshape(bs, 1, D_MODEL)

    new_k_prompt = k_tail.reshape(1, bp, WINDOW, N_KV, HEAD_DIM)
    new_v_prompt = v_tail.reshape(1, bp, WINDOW, N_KV, HEAD_DIM)
    new_pool_prompt = u_tail[:, N_META - POOL_HIST:][None]
    return (y_prompt, y_sample, new_k_prompt, new_v_prompt, new_pool_prompt,
            nk_s.reshape(1, bs, wk, N_KV, HEAD_DIM), nv_s.reshape(1, bs, wk, N_KV, HEAD_DIM), npool_s.swapaxes(0, 1)[None])
```

```python
import functools
import math

import jax
import numpy as np
import jax.numpy as jnp
from jax import lax
from jax.experimental import pallas as pl
from jax.experimental.pallas import tpu as pltpu

D_MODEL = 2048
N_META = 16
HEAD_DIM = 64
N_KV = 4
GQA = 4
Q_COLS = N_KV * GQA * HEAD_DIM
KV_COLS = N_KV * HEAD_DIM
POOL_W = 1024
POOL_WINDOWS = (2, 4, 8, 16)
POOL_GW = POOL_W // len(POOL_WINDOWS)
POOL_HIST = 15
IN_COLS = Q_COLS + 2 * KV_COLS + POOL_W
D_FF = 5632
WINDOW = 128
BLOCK = 128
ROT_DIM = 16
ROPE_THETA = 500000.0
PAST_LEN = 16384
ALPHA = 2.0 ** 0.25
LN_EPS = 1e-5
NEG_INF = -1e30
LOG2E = math.log2(math.e)
QK_SCALE = LOG2E / math.sqrt(HEAD_DIM)
V7X_MXU_COLS = 256
FFN_SUB = V7X_MXU_COLS
ROW_SUB = 256

V7X_VMEM_LIMIT = 62 * 1024 * 1024

TM_INPROJ = 512
TM_MIXER = 512
TM_FFN = 1024
TF_FFN = 512
BB_SAMPLE = 32

BF16 = jnp.bfloat16
F32 = jnp.float32


def _layer_norm(x, g, b):
    mu = jnp.mean(x, axis=-1, keepdims=True)
    xc = x - mu
    var = jnp.mean(xc * xc, axis=-1, keepdims=True)
    return xc * lax.rsqrt(var + LN_EPS) * g + b


def _params(sem):
    return pltpu.CompilerParams(dimension_semantics=sem, vmem_limit_bytes=V7X_VMEM_LIMIT)


def _cast_specs(cast, steps):
    specs = []
    for a in cast:
        span = 1
        while a.shape[0] % (steps // span) or (a.shape[0] // (steps // span)) % 16:
            span *= 2
        specs.append(pl.BlockSpec((a.shape[0] // (steps // span), a.shape[1]),
                                  functools.partial(lambda i, sp: (i // sp, 0), sp=span)))
    return specs


def _cast_slabs(cast_in, cast_out):
    for src_ref, dst_ref in zip(cast_in, cast_out):
        dst_ref[...] = src_ref[...].astype(BF16)


def _swap_head_order(cols):
    low = lax.broadcasted_iota(jnp.int32, (1, 128), 1) < HEAD_DIM
    out = []
    for j in range(2 * N_KV):
        b, a0 = j // 2, 2 * (j % 2)
        x = cols[(a0 * 4 + b) // 2]
        y = cols[((a0 + 1) * 4 + b) // 2]
        if b % 2 == 0:
            out.append(jnp.where(low, x, pltpu.roll(y, HEAD_DIM, 1)))
        else:
            out.append(jnp.where(low, pltpu.roll(x, HEAD_DIM, 1), y))
    return out


def _inproj_kernel(x_ref, g_ref, b_ref, w_ref, bias_ref, c_ref, s1_ref, s2_ref, *rest, n_cast):
    cast_in, (h_ref, q_ref, k_ref, v_ref, u_ref) = rest[:n_cast], rest[n_cast:n_cast + 5]
    cast_out, tails = rest[n_cast + 5:2 * n_cast + 5], rest[2 * n_cast + 5:]
    _cast_slabs(cast_in, cast_out)
    tm = x_ref.shape[0]
    sub = min(tm, ROW_SUB)
    for r in range(tm // sub):
        rows = slice(r * sub, (r + 1) * sub)
        h = _layer_norm(x_ref[rows, :], g_ref[...], b_ref[...])
        h_ref[rows, :] = h
        z = jnp.dot(h.astype(BF16), w_ref[...], preferred_element_type=F32) + bias_ref[...]
        cos = c_ref[rows, :]
        s_lo = s1_ref[rows, :]
        s_hi = s2_ref[rows, :]

        def rope(zc):
            return (zc * cos + pltpu.roll(zc, 128 - ROT_DIM // 2, 1) * s_lo
                    + pltpu.roll(zc, ROT_DIM // 2, 1) * s_hi)

        q_cols = _swap_head_order([rope(z[:, c * 128:(c + 1) * 128]) for c in range(Q_COLS // 128)])
        for c, qc in enumerate(q_cols):
            q_ref[rows, c * 128:(c + 1) * 128] = (qc * QK_SCALE).astype(BF16)
        k = jnp.concatenate([rope(z[:, Q_COLS + c * 128:Q_COLS + (c + 1) * 128]) for c in range(KV_COLS // 128)], 1)
        v = z[:, Q_COLS + KV_COLS:Q_COLS + 2 * KV_COLS]
        u = z[:, Q_COLS + 2 * KV_COLS:]
        k_ref[rows, :] = k.astype(k_ref.dtype)
        v_ref[rows, :] = v.astype(v_ref.dtype)
        u_ref[rows, :] = u
        if tails and r == tm // sub - 1:
            k_tail, v_tail, u_tail = tails
            k_tail[...] = k[sub - WINDOW:]
            v_tail[...] = v[sub - WINDOW:]
            u_tail[...] = u[sub - N_META:]


def _inproj(x, ln_g, ln_b, w_in, b_in, tabs, tm, cast=(), seq_steps=None):
    rows = x.shape[0]
    steps = rows // tm
    kv_dtype = F32 if seq_steps is None else BF16
    tail_specs, tail_shapes = [], []
    if seq_steps is not None:
        n_seq = steps // seq_steps
        seq_of = lambda i: (i // seq_steps, 0, 0)
        for n_rows, width in ((WINDOW, KV_COLS), (WINDOW, KV_COLS), (N_META, POOL_W)):
            tail_specs.append(pl.BlockSpec((None, n_rows, width), seq_of))
            tail_shapes.append(jax.ShapeDtypeStruct((n_seq, n_rows, width), F32))
    t_rows = tabs[0].shape[0]
    n_t = t_rows // tm
    row = lambda i: (i, 0)
    fixed = lambda i: (0, 0)
    tab = lambda i: (i % n_t, 0)
    cast_specs = _cast_specs(cast, steps)
    return pl.pallas_call(
        functools.partial(_inproj_kernel, n_cast=len(cast)),
        grid=(steps,),
        in_specs=[
            pl.BlockSpec((tm, D_MODEL), row),
            pl.BlockSpec((1, D_MODEL), fixed),
            pl.BlockSpec((1, D_MODEL), fixed),
            pl.BlockSpec((D_MODEL, IN_COLS), fixed),
            pl.BlockSpec((1, IN_COLS), fixed),
            pl.BlockSpec((tm, 128), tab),
            pl.BlockSpec((tm, 128), tab),
            pl.BlockSpec((tm, 128), tab),
        ] + cast_specs,
        out_specs=[
            pl.BlockSpec((tm, D_MODEL), row),
            pl.BlockSpec((tm, Q_COLS), row),
            pl.BlockSpec((tm, KV_COLS), row),
            pl.BlockSpec((tm, KV_COLS), row),
            pl.BlockSpec((tm, POOL_W), row),
        ] + cast_specs + tail_specs,
        out_shape=[
            jax.ShapeDtypeStruct((rows, D_MODEL), F32),
            jax.ShapeDtypeStruct((rows, Q_COLS), BF16),
            jax.ShapeDtypeStruct((rows, KV_COLS), kv_dtype),
            jax.ShapeDtypeStruct((rows, KV_COLS), kv_dtype),
            jax.ShapeDtypeStruct((rows, POOL_W), F32),
        ] + [jax.ShapeDtypeStruct(a.shape, BF16) for a in cast] + tail_shapes,
        compiler_params=_params(("arbitrary",)),
        name="inproj",
    )(x, ln_g, ln_b, w_in, b_in, *tabs, *cast)


def _outproj_ln(mix_bf16, h, wo_ref, g_ref, b_ref):
    mix = jnp.dot(mix_bf16, wo_ref[...], preferred_element_type=F32)
    return _layer_norm(ALPHA * h + mix, g_ref[...], b_ref[...])


def _pool_group_matmul(d, g, wpool_ref, pscale_ref):
    cs = slice(g * POOL_GW, (g + 1) * POOL_GW)
    y = jnp.dot(d.astype(BF16), wpool_ref[g], preferred_element_type=F32)
    return (y * pscale_ref[:, cs]).astype(BF16)


def _prompt_mixer_kernel(sinks_ref, q_ref, k_ref, khalo_ref, kmeta_ref, v_ref, vhalo_ref, vmeta_ref,
                         u_ref, uhalo_ref, umeta_ref, wpool_ref, pscale_ref, hprev_ref, wo_ref, g_ref, b_ref,
                         h1_ref, kx_ref, vx_ref, ux_ref, mix_ref, wos_ref, *, tm, n_j, n_tiles):
    t = pl.program_id(0)
    j = jnp.minimum(t, n_tiles - 1) % n_j
    slot = t % 2
    n_blk = tm // BLOCK

    @pl.when(t == 0)
    def _():
        wos_ref[...] = wo_ref[...]
        mix_ref[...] = jnp.zeros(mix_ref.shape, BF16)

    n_col = D_MODEL // V7X_MXU_COLS
    pieces = [(rb, nc) for rb in range(tm // ROW_SUB) for nc in range(n_col)]
    assert len(pieces) == n_blk * N_KV
    parts = []

    def outproj_piece(idx):
        rb, nc = pieces[idx]
        rows = slice(rb * ROW_SUB, (rb + 1) * ROW_SUB)
        cols = slice(nc * V7X_MXU_COLS, (nc + 1) * V7X_MXU_COLS)
        parts.append(jnp.dot(mix_ref[1 - slot, rows, :], wos_ref[:, cols], preferred_element_type=F32))
        if nc == n_col - 1:
            proj = jnp.concatenate(parts, axis=1)
            parts.clear()
            h1_ref[rows, :] = _layer_norm(ALPHA * hprev_ref[rows, :] + proj, g_ref[...], b_ref[...])

    lane_head = lax.broadcasted_iota(jnp.int32, (1, KV_COLS), 1) // HEAD_DIM
    pad = BLOCK - N_META

    @pl.when(j == 0)
    def _():
        zeros = jnp.zeros((pad, KV_COLS), BF16)
        kx_ref[0:pad, :] = zeros
        vx_ref[0:pad, :] = zeros
        kx_ref[pad:BLOCK, :] = kmeta_ref[...].astype(BF16)
        vx_ref[pad:BLOCK, :] = vmeta_ref[...].astype(BF16)
        ux_ref[0:N_META, :] = umeta_ref[...]

    @pl.when(j > 0)
    def _():
        kx_ref[0:BLOCK, :] = khalo_ref[...]
        vx_ref[0:BLOCK, :] = vhalo_ref[...]
        ux_ref[0:N_META, :] = uhalo_ref[...]

    kx_ref[BLOCK:, :] = k_ref[...]
    vx_ref[BLOCK:, :] = v_ref[...]
    ux_ref[N_META:, :] = u_ref[...]

    n_rows = GQA * BLOCK
    qi = lax.broadcasted_iota(jnp.int32, (n_rows, 2 * BLOCK), 0) % BLOCK
    ki = lax.broadcasted_iota(jnp.int32, (n_rows, 2 * BLOCK), 1)
    band = (ki >= qi) & (ki <= qi + WINDOW)
    row_group = lax.broadcasted_iota(jnp.int32, (n_rows, 1), 0) // BLOCK
    sink_cols = []
    for hh in range(N_KV):
        sink = jnp.zeros((n_rows, 1), F32)
        for gg in range(GQA):
            sink = jnp.where(row_group == gg, sinks_ref[hh * GQA + gg] * LOG2E, sink)
        sink_cols.append(sink)

    for blk in range(n_blk):
        r0 = blk * BLOCK
        rows = slice(r0, r0 + BLOCK)

        e = ux_ref[r0:r0 + N_META + BLOCK, :]
        sums = []
        cur = e
        for step in (1, 2, 4, 8):
            cur = cur + pltpu.roll(cur, step, 0)
            sums.append(cur[:, :POOL_GW])
            cur = cur[:, POOL_GW:]
        pooled = []
        for g, w in enumerate(POOL_WINDOWS):
            cs = slice(g * POOL_GW, (g + 1) * POOL_GW)
            d = sums[g][N_META:] * (1.0 / w) - e[N_META:, cs]
            pooled.append(_pool_group_matmul(d, g, wpool_ref, pscale_ref))

        if blk == 0:
            mask = band & (ki >= jnp.where(j == 0, pad, 0))
        else:
            mask = band
        qs = jnp.concatenate([q_ref[rows, gg * KV_COLS:(gg + 1) * KV_COLS] for gg in range(GQA)], axis=0)
        o = jnp.zeros((n_rows, KV_COLS), F32)
        for hh in range(N_KV):
            kb = jnp.where(lane_head == hh, kx_ref[r0:r0 + 2 * BLOCK, :], 0)
            vb = jnp.where(lane_head == hh, vx_ref[r0:r0 + 2 * BLOCK, :], 0)
            s = lax.dot_general(qs, kb, (((1,), (1,)), ((), ())), preferred_element_type=F32)
            s = jnp.where(mask, s, NEG_INF)
            sink = sink_cols[hh]
            m = jnp.max(s, axis=-1, keepdims=True)
            p = jnp.exp2(s - m)
            denom = jnp.sum(p, axis=-1, keepdims=True) + jnp.exp2(sink - m)
            pn = (p * (1.0 / denom)).astype(BF16)
            o = o + jnp.dot(pn, vb, preferred_element_type=F32)
            outproj_piece(blk * N_KV + hh)
        attn = _swap_head_order([o[gg * BLOCK:(gg + 1) * BLOCK, c * 128:(c + 1) * 128]
                                 for gg in range(GQA) for c in range(KV_COLS // 128)])
        mix_ref[slot, rows, :] = jnp.concatenate([a.astype(BF16) for a in attn] + pooled, axis=1)


def _prompt_mixer(sinks, q, k, v, u, h, kmeta, vmeta, umeta, wpool, pscale, wo, ln_g, ln_b, tm):
    bsz, seq = q.shape[0], q.shape[1]
    n_j = seq // tm
    n_tiles = bsz * n_j
    cur = lambda t: jnp.minimum(t, n_tiles - 1)
    tile = lambda t: (cur(t) // n_j, cur(t) % n_j, 0)
    fixed2 = lambda t: (0, 0)
    fixed3 = lambda t: (0, 0, 0)
    kv_halo = lambda t: (cur(t) // n_j, jnp.maximum((cur(t) % n_j) * (tm // BLOCK) - 1, 0), 0)
    u_halo = lambda t: (cur(t) // n_j, jnp.maximum((cur(t) % n_j) * (tm // N_META) - 1, 0), 0)
    prev = lambda t: (jnp.maximum(t - 1, 0), 0)
    return pl.pallas_call(
        functools.partial(_prompt_mixer_kernel, tm=tm, n_j=n_j, n_tiles=n_tiles),
        grid=(n_tiles + 1,),
        in_specs=[
            pl.BlockSpec(memory_space=pltpu.SMEM),
            pl.BlockSpec((None, tm, Q_COLS), tile),
            pl.BlockSpec((None, tm, KV_COLS), tile),
            pl.BlockSpec((None, BLOCK, KV_COLS), kv_halo),
            pl.BlockSpec((N_META, KV_COLS), fixed2),
            pl.BlockSpec((None, tm, KV_COLS), tile),
            pl.BlockSpec((None, BLOCK, KV_COLS), kv_halo),
            pl.BlockSpec((N_META, KV_COLS), fixed2),
            pl.BlockSpec((None, tm, POOL_W), tile),
            pl.BlockSpec((None, N_META, POOL_W), u_halo),
            pl.BlockSpec((N_META, POOL_W), fixed2),
            pl.BlockSpec((len(POOL_WINDOWS), POOL_GW, POOL_GW), fixed3),
            pl.BlockSpec((1, POOL_W), fixed2),
            pl.BlockSpec((tm, D_MODEL), prev),
            pl.BlockSpec((D_MODEL, D_MODEL), fixed2, pipeline_mode=pl.Buffered(1)),
            pl.BlockSpec((1, D_MODEL), fixed2),
            pl.BlockSpec((1, D_MODEL), fixed2),
        ],
        out_specs=pl.BlockSpec((tm, D_MODEL), prev),
        out_shape=jax.ShapeDtypeStruct((bsz * seq, D_MODEL), F32),
        scratch_shapes=[
            pltpu.VMEM((BLOCK + tm, KV_COLS), BF16),
            pltpu.VMEM((BLOCK + tm, KV_COLS), BF16),
            pltpu.VMEM((N_META + tm, POOL_W), F32),
            pltpu.VMEM((2, tm, D_MODEL), BF16),
            pltpu.VMEM((D_MODEL, D_MODEL), BF16),
        ],
        compiler_params=_params(("arbitrary",)),
        name="prompt_mixer",
    )(sinks, q, k, k, kmeta, v, v, vmeta, u, u, umeta, wpool, pscale, h, wo, ln_g, ln_b)


def _sample_attn_kernel(sinks_ref, q_ref, kn_ref, vn_ref, ck_ref, cv_ref,
                        attn_ref, nk_ref, nv_ref):
    bb = q_ref.shape[0]
    wk = ck_ref.shape[1]
    ck = ck_ref[...]
    cv = cv_ref[...]
    kn = kn_ref[...]
    vn = vn_ref[...]
    head_of_lane = lax.broadcasted_iota(jnp.int32, (KV_COLS, 128), 0) // HEAD_DIM
    out_lane = lax.broadcasted_iota(jnp.int32, (KV_COLS, 128), 1)
    seg = (head_of_lane == out_lane).astype(BF16)
    lane_t = lax.broadcasted_iota(jnp.int32, (128, KV_COLS), 0)
    head_t = lax.broadcasted_iota(jnp.int32, (128, KV_COLS), 1) // HEAD_DIM
    spread = (lane_t == head_t).astype(BF16)
    lane = lax.broadcasted_iota(jnp.int32, (1, 128), 1)

    o_cols = []
    for gg in range(GQA):
        qg = q_ref[:, gg * KV_COLS:(gg + 1) * KV_COLS].astype(F32)
        prod = (ck * qg[:, None, :]).astype(BF16).reshape(bb * wk, KV_COLS)
        s = jnp.dot(prod, seg, preferred_element_type=F32).reshape(bb, wk, 128)
        s_new = jnp.dot((kn * qg).astype(BF16), seg, preferred_element_type=F32)
        sink = jnp.zeros((1, 128), F32)
        for hh in range(N_KV):
            sink = jnp.where(lane == hh, sinks_ref[hh * GQA + gg] * LOG2E, sink)
        m = jnp.maximum(jnp.maximum(jnp.max(s, axis=1), s_new), sink)
        p = jnp.exp2(s - m[:, None, :])
        p_new = jnp.exp2(s_new - m)
        denom = jnp.sum(p, axis=1) + p_new + jnp.exp2(sink - m)
        r = 1.0 / denom
        pn = (p * r[:, None, :]).astype(BF16).reshape(bb * wk, 128)
        pn_new = (p_new * r).astype(BF16)
        pe = jnp.dot(pn, spread, preferred_element_type=F32).reshape(bb, wk, KV_COLS)
        pe_new = jnp.dot(pn_new, spread, preferred_element_type=F32)
        o = jnp.sum(pe * cv, axis=1) + pe_new * vn
        o_cols += [o[:, c * 128:(c + 1) * 128] for c in range(KV_COLS // 128)]
    for c, oc in enumerate(_swap_head_order(o_cols)):
        attn_ref[:, c * 128:(c + 1) * 128] = oc.astype(BF16)

    nk_ref[:, 0:wk - 1, :] = ck_ref[:, 1:wk, :]
    nk_ref[:, wk - 1:wk, :] = kn[:, None, :]
    nv_ref[:, 0:wk - 1, :] = cv_ref[:, 1:wk, :]
    nv_ref[:, wk - 1:wk, :] = vn[:, None, :]


def _sample_attn(sinks, q, kn, vn, ck, cv, bb):
    bsz, wk = ck.shape[0], ck.shape[1]
    row = lambda i: (i, 0)
    cache = lambda i: (i, 0, 0)
    return pl.pallas_call(
        _sample_attn_kernel,
        grid=(bsz // bb,),
        in_specs=[
            pl.BlockSpec(memory_space=pltpu.SMEM),
            pl.BlockSpec((bb, Q_COLS), row),
            pl.BlockSpec((bb, KV_COLS), row),
            pl.BlockSpec((bb, KV_COLS), row),
            pl.BlockSpec((bb, wk, KV_COLS), cache),
            pl.BlockSpec((bb, wk, KV_COLS), cache),
        ],
        out_specs=[
            pl.BlockSpec((bb, Q_COLS), row),
            pl.BlockSpec((bb, wk, KV_COLS), cache),
            pl.BlockSpec((bb, wk, KV_COLS), cache),
        ],
        out_shape=[
            jax.ShapeDtypeStruct((bsz, Q_COLS), BF16),
            jax.ShapeDtypeStruct((bsz, wk, KV_COLS), F32),
            jax.ShapeDtypeStruct((bsz, wk, KV_COLS), F32),
        ],
        compiler_params=_params(("parallel",)),
        name="sample_attn",
    )(sinks, q, kn, vn, ck, cv)


def _sample_mix_kernel(attn_ref, u_ref, st_ref, h_ref, wpool_ref, pscale_ref, wo_ref, g_ref, b_ref,
                       h1_ref, nst_ref, mix_ref):
    u = u_ref[...]
    mix_ref[:, 0:Q_COLS] = attn_ref[...]
    for g, w in enumerate(POOL_WINDOWS):
        cs = slice(g * POOL_GW, (g + 1) * POOL_GW)
        acc = u[:, cs]
        for t in range(POOL_HIST - (w - 1), POOL_HIST):
            acc = acc + st_ref[t, :, cs]
        d = acc * (1.0 / w) - u[:, cs]
        mix_ref[:, Q_COLS + g * POOL_GW:Q_COLS + (g + 1) * POOL_GW] = _pool_group_matmul(d, g, wpool_ref, pscale_ref)
    nst_ref[0:POOL_HIST - 1] = st_ref[1:POOL_HIST]
    nst_ref[POOL_HIST - 1] = u
    h1_ref[...] = _outproj_ln(mix_ref[...], h_ref[...], wo_ref, g_ref, b_ref)


def _sample_mix(attn, u, state, h, wpool, pscale, wo, ln_g, ln_b):
    bsz = attn.shape[0]
    return pl.pallas_call(
        _sample_mix_kernel,
        out_shape=[
            jax.ShapeDtypeStruct((bsz, D_MODEL), F32),
            jax.ShapeDtypeStruct((POOL_HIST, bsz, POOL_W), F32),
        ],
        scratch_shapes=[pltpu.VMEM((bsz, D_MODEL), BF16)],
        compiler_params=pltpu.CompilerParams(vmem_limit_bytes=V7X_VMEM_LIMIT),
        name="sample_mix",
    )(attn, u, state, h, wpool, pscale, wo, ln_g, ln_b)


def _ffn_kernel(h_ref, wg_ref, wu_ref, wd_ref, g_ref, b_ref, o_ref):
    f = pl.program_id(1)
    last = pl.num_programs(1) - 1
    tm = h_ref.shape[0]
    tf = wg_ref.shape[1]

    @pl.when(f == 0)
    def _():
        o_ref[...] = ALPHA * h_ref[...]

    def activation(hb, c):
        cs = slice(c * FFN_SUB, (c + 1) * FFN_SUB)
        gate = jnp.dot(hb, wg_ref[:, cs], preferred_element_type=F32)
        up = jnp.dot(hb, wu_ref[:, cs], preferred_element_type=F32)
        return (gate * (1.0 / (1.0 + jnp.exp(-gate))) * up).astype(BF16)

    @pl.when(f < last)
    def _():
        hb = h_ref[...].astype(BF16)
        part = None
        for c in range(tf // FFN_SUB):
            p = jnp.dot(activation(hb, c), wd_ref[c * FFN_SUB:(c + 1) * FFN_SUB, :], preferred_element_type=F32)
            part = p if part is None else part + p
        o_ref[...] += part

    @pl.when(f == last)
    def _():
        hb = h_ref[...].astype(BF16)
        a = jnp.concatenate([activation(hb, c) for c in range(tf // FFN_SUB)], axis=1)
        sub = min(tm, ROW_SUB)
        for r in range(tm // sub):
            rows = slice(r * sub, (r + 1) * sub)
            p = jnp.dot(a[rows], wd_ref[...], preferred_element_type=F32)
            o_ref[rows, :] = _layer_norm(o_ref[rows, :] + p, g_ref[...], b_ref[...])


def _ffn(h1, wg, wu, wd, ln_g, ln_b, tm, tf):
    rows = h1.shape[0]
    return pl.pallas_call(
        _ffn_kernel,
        grid=(rows // tm, D_FF // tf),
        in_specs=[
            pl.BlockSpec((tm, D_MODEL), lambda i, f: (i, 0)),
            pl.BlockSpec((D_MODEL, tf), lambda i, f: (0, f)),
            pl.BlockSpec((D_MODEL, tf), lambda i, f: (0, f)),
            pl.BlockSpec((tf, D_MODEL), lambda i, f: (f, 0)),
            pl.BlockSpec((1, D_MODEL), lambda i, f: (0, 0)),
            pl.BlockSpec((1, D_MODEL), lambda i, f: (0, 0)),
        ],
        out_specs=pl.BlockSpec((tm, D_MODEL), lambda i, f: (i, 0)),
        out_shape=jax.ShapeDtypeStruct((rows, D_MODEL), F32),
        compiler_params=_params(("parallel", "arbitrary")),
        name="ffn",
    )(h1, wg, wu, wd, ln_g, ln_b)


def _rope_tables(pos):
    half = ROT_DIM // 2
    inv_freq = ROPE_THETA ** (-np.arange(half, dtype=np.float64) * 2.0 / ROT_DIM)
    ang = np.asarray(pos, np.float64)[:, None] * inv_freq
    cos, sin = np.cos(ang), np.sin(ang)
    t = ang.shape[0]
    ones = np.ones((t, HEAD_DIM - ROT_DIM))
    zeros = np.zeros((t, HEAD_DIM - ROT_DIM))
    zh = np.zeros((t, half))
    c = np.concatenate([cos, cos, ones], -1)
    s_lo = np.concatenate([-sin, zh, zeros], -1)
    s_hi = np.concatenate([zh, sin, zeros], -1)
    return tuple(jnp.asarray(np.concatenate([a, a], -1), F32) for a in (c, s_lo, s_hi))


def kernel(x_prompt, x_sample, cache_k, cache_v, state_pool, meta_tokens, ln_in_g, ln_in_b, w_in, b_in,
           attn_sinks, w_pool, pool_scale, w_o, ln1_g, ln1_b, w_gate, w_up, w_down, ln2_g, ln2_b):
    bp, seq, _ = x_prompt.shape
    bs = x_sample.shape[0]
    wk = cache_k.shape[2]
    assert x_sample.shape[1] == 1 and cache_k.shape[0] == 1

    w_in_p = w_in[0].astype(BF16)
    b_in_p = b_in[0][None]
    wpool = w_pool[0].astype(BF16)
    pscale = pool_scale[0][None]
    sinks = attn_sinks[0]
    g_in, b_ln_in = ln_in_g[None], ln_in_b[None]
    g1, b1, g2, b2 = ln1_g[0][None], ln1_b[0][None], ln2_g[0][None], ln2_b[0][None]

    tabs_prompt = _rope_tables(N_META + np.arange(seq))
    tabs_small = _rope_tables(np.concatenate([np.arange(N_META), np.full((bs,), PAST_LEN)]))

    x_small = jnp.concatenate([meta_tokens, x_sample.reshape(bs, D_MODEL)], 0)
    h_ms, q_ms, k_ms, v_ms, u_ms = _inproj(x_small, g_in, b_ln_in, w_in_p, b_in_p, tabs_small, N_META + bs)
    k_meta, v_meta, u_meta = k_ms[:N_META], v_ms[:N_META], u_ms[:N_META]
    h_s, q_s, k_s, v_s, u_s = h_ms[N_META:], q_ms[N_META:], k_ms[N_META:], v_ms[N_META:], u_ms[N_META:]

    h_p, q_p, k_p, v_p, u_p, wg, wu, wd, w_o_p, k_tail, v_tail, u_tail = _inproj(
        x_prompt.reshape(bp * seq, D_MODEL), g_in, b_ln_in, w_in_p, b_in_p, tabs_prompt, TM_INPROJ,
        cast=(w_gate[0], w_up[0], w_down[0], w_o[0]), seq_steps=seq // TM_INPROJ)
    k_p3 = k_p.reshape(bp, seq, KV_COLS)
    v_p3 = v_p.reshape(bp, seq, KV_COLS)
    u_p3 = u_p.reshape(bp, seq, POOL_W)
    h1_p = _prompt_mixer(sinks, q_p.reshape(bp, seq, Q_COLS), k_p3, v_p3, u_p3, h_p,
                         k_meta, v_meta, u_meta, wpool, pscale, w_o_p, g1, b1, TM_MIXER)
    y_prompt = _ffn(h1_p, wg, wu, wd, g2, b2, TM_FFN, TF_FFN).reshape(bp, seq, D_MODEL)

    attn_s, nk_s, nv_s = _sample_attn(sinks, q_s, k_s, v_s, cache_k.reshape(bs, wk, KV_COLS),
                                      cache_v.reshape(bs, wk, KV_COLS), BB_SAMPLE)
    h1_s, npool_s = _sample_mix(attn_s, u_s, state_pool[0].swapaxes(0, 1), h_s, wpool, pscale, w_o_p, g1, b1)
    y_sample = _ffn(h1_s, wg, wu, wd, g2, b2, bs, TF_FFN).reshape(bs, 1, D_MODEL)

    new_k_prompt = k_tail.reshape(1, bp, WINDOW, N_KV, HEAD_DIM)
    new_v_prompt = v_tail.reshape(1, bp, WINDOW, N_KV, HEAD_DIM)
    new_pool_prompt = u_tail[:, N_META - POOL_HIST:][None]
    return (y_prompt, y_sample, new_k_prompt, new_v_prompt, new_pool_prompt,
            nk_s.reshape(1, bs, wk, N_KV, HEAD_DIM), nv_s.reshape(1, bs, wk, N_KV, HEAD_DIM), npool_s.swapaxes(0, 1)[None])
```

```python
import functools
import math

import jax
import numpy as np
import jax.numpy as jnp
from jax import lax
from jax.experimental import pallas as pl
from jax.experimental.pallas import tpu as pltpu

D_MODEL = 2048
N_META = 16
HEAD_DIM = 64
N_KV = 4
GQA = 4
Q_COLS = N_KV * GQA * HEAD_DIM
KV_COLS = N_KV * HEAD_DIM
POOL_W = 1024
POOL_WINDOWS = (2, 4, 8, 16)
POOL_GW = POOL_W // len(POOL_WINDOWS)
POOL_HIST = 15
IN_COLS = Q_COLS + 2 * KV_COLS + POOL_W
D_FF = 5632
WINDOW = 128
BLOCK = 128
ROT_DIM = 16
ROPE_THETA = 500000.0
PAST_LEN = 16384
ALPHA = 2.0 ** 0.25
LN_EPS = 1e-5
NEG_INF = -1e30
LOG2E = math.log2(math.e)
QK_SCALE = LOG2E / math.sqrt(HEAD_DIM)
V7X_MXU_COLS = 256
FFN_SUB = V7X_MXU_COLS
ROW_SUB = 256

V7X_VMEM_LIMIT = 62 * 1024 * 1024

TM_INPROJ = 512
TM_MIXER = 512
TM_FFN = 1024
TF_FFN = 512
BB_SAMPLE = 32

BF16 = jnp.bfloat16
F32 = jnp.float32


def _layer_norm(x, g, b):
    mu = jnp.mean(x, axis=-1, keepdims=True)
    xc = x - mu
    var = jnp.mean(xc * xc, axis=-1, keepdims=True)
    return xc * lax.rsqrt(var + LN_EPS) * g + b


def _params(sem):
    return pltpu.CompilerParams(dimension_semantics=sem, vmem_limit_bytes=V7X_VMEM_LIMIT)


def _cast_specs(cast, steps):
    specs = []
    for a in cast:
        span = 1
        while a.shape[0] % (steps // span) or (a.shape[0] // (steps // span)) % 16:
            span *= 2
        specs.append(pl.BlockSpec((a.shape[0] // (steps // span), a.shape[1]),
                                  functools.partial(lambda i, sp: (i // sp, 0), sp=span)))
    return specs


def _cast_slabs(cast_in, cast_out):
    for src_ref, dst_ref in zip(cast_in, cast_out):
        dst_ref[...] = src_ref[...].astype(BF16)


def _swap_head_order(cols):
    low = lax.broadcasted_iota(jnp.int32, (1, 128), 1) < HEAD_DIM
    out = []
    for j in range(2 * N_KV):
        b, a0 = j // 2, 2 * (j % 2)
        x = cols[(a0 * 4 + b) // 2]
        y = cols[((a0 + 1) * 4 + b) // 2]
        if b % 2 == 0:
            out.append(jnp.where(low, x, pltpu.roll(y, HEAD_DIM, 1)))
        else:
            out.append(jnp.where(low, pltpu.roll(x, HEAD_DIM, 1), y))
    return out


def _inproj_kernel(x_ref, g_ref, b_ref, w_ref, bias_ref, c_ref, s1_ref, s2_ref, *rest, n_cast):
    cast_in, (h_ref, q_ref, k_ref, v_ref, u_ref) = rest[:n_cast], rest[n_cast:n_cast + 5]
    cast_out, tails = rest[n_cast + 5:2 * n_cast + 5], rest[2 * n_cast + 5:]
    _cast_slabs(cast_in, cast_out)
    tm = x_ref.shape[0]
    sub = min(tm, ROW_SUB)
    for r in range(tm // sub):
        rows = slice(r * sub, (r + 1) * sub)
        h = _layer_norm(x_ref[rows, :], g_ref[...], b_ref[...])
        h_ref[rows, :] = h
        z = jnp.dot(h.astype(BF16), w_ref[...], preferred_element_type=F32) + bias_ref[...]
        cos = c_ref[rows, :]
        s_lo = s1_ref[rows, :]
        s_hi = s2_ref[rows, :]

        def rope(zc):
            return (zc * cos + pltpu.roll(zc, 128 - ROT_DIM // 2, 1) * s_lo
                    + pltpu.roll(zc, ROT_DIM // 2, 1) * s_hi)

        q_cols = _swap_head_order([rope(z[:, c * 128:(c + 1) * 128]) for c in range(Q_COLS // 128)])
        for c, qc in enumerate(q_cols):
            q_ref[rows, c * 128:(c + 1) * 128] = (qc * QK_SCALE).astype(BF16)
        k = jnp.concatenate([rope(z[:, Q_COLS + c * 128:Q_COLS + (c + 1) * 128]) for c in range(KV_COLS // 128)], 1)
        v = z[:, Q_COLS + KV_COLS:Q_COLS + 2 * KV_COLS]
        u = z[:, Q_COLS + 2 * KV_COLS:]
        k_ref[rows, :] = k.astype(k_ref.dtype)
        v_ref[rows, :] = v.astype(v_ref.dtype)
        u_ref[rows, :] = u
        if tails and r == tm // sub - 1:
            k_tail, v_tail, u_tail = tails
            k_tail[...] = k[sub - WINDOW:]
            v_tail[...] = v[sub - WINDOW:]
            u_tail[...] = u[sub - N_META:]


def _inproj(x, ln_g, ln_b, w_in, b_in, tabs, tm, cast=(), seq_steps=None):
    rows = x.shape[0]
    steps = rows // tm
    kv_dtype = F32 if seq_steps is None else BF16
    tail_specs, tail_shapes = [], []
    if seq_steps is not None:
        n_seq = steps // seq_steps
        seq_of = lambda i: (i // seq_steps, 0, 0)
        for n_rows, width in ((WINDOW, KV_COLS), (WINDOW, KV_COLS), (N_META, POOL_W)):
            tail_specs.append(pl.BlockSpec((None, n_rows, width), seq_of))
            tail_shapes.append(jax.ShapeDtypeStruct((n_seq, n_rows, width), F32))
    t_rows = tabs[0].shape[0]
    n_t = t_rows // tm
    row = lambda i: (i, 0)
    fixed = lambda i: (0, 0)
    tab = lambda i: (i % n_t, 0)
    cast_specs = _cast_specs(cast, steps)
    return pl.pallas_call(
        functools.partial(_inproj_kernel, n_cast=len(cast)),
        grid=(steps,),
        in_specs=[
            pl.BlockSpec((tm, D_MODEL), row),
            pl.BlockSpec((1, D_MODEL), fixed),
            pl.BlockSpec((1, D_MODEL), fixed),
            pl.BlockSpec((D_MODEL, IN_COLS), fixed),
            pl.BlockSpec((1, IN_COLS), fixed),
            pl.BlockSpec((tm, 128), tab),
            pl.BlockSpec((tm, 128), tab),
            pl.BlockSpec((tm, 128), tab),
        ] + cast_specs,
        out_specs=[
            pl.BlockSpec((tm, D_MODEL), row),
            pl.BlockSpec((tm, Q_COLS), row),
            pl.BlockSpec((tm, KV_COLS), row),
            pl.BlockSpec((tm, KV_COLS), row),
            pl.BlockSpec((tm, POOL_W), row),
        ] + cast_specs + tail_specs,
        out_shape=[
            jax.ShapeDtypeStruct((rows, D_MODEL), F32),
            jax.ShapeDtypeStruct((rows, Q_COLS), BF16),
            jax.ShapeDtypeStruct((rows, KV_COLS), kv_dtype),
            jax.ShapeDtypeStruct((rows, KV_COLS), kv_dtype),
            jax.ShapeDtypeStruct((rows, POOL_W), F32),
        ] + [jax.ShapeDtypeStruct(a.shape, BF16) for a in cast] + tail_shapes,
        compiler_params=_params(("arbitrary",)),
        name="inproj",
    )(x, ln_g, ln_b, w_in, b_in, *tabs, *cast)


def _outproj_ln(mix_bf16, h, wo_ref, g_ref, b_ref):
    mix = jnp.dot(mix_bf16, wo_ref[...], preferred_element_type=F32)
    return _layer_norm(ALPHA * h + mix, g_ref[...], b_ref[...])


def _pool_group_matmul(d, g, wpool_ref, pscale_ref):
    cs = slice(g * POOL_GW, (g + 1) * POOL_GW)
    y = jnp.dot(d.astype(BF16), wpool_ref[g], preferred_element_type=F32)
    return (y * pscale_ref[:, cs]).astype(BF16)


def _prompt_mixer_kernel(sinks_ref, q_ref, k_ref, khalo_ref, kmeta_ref, v_ref, vhalo_ref, vmeta_ref,
                         u_ref, uhalo_ref, umeta_ref, wpool_ref, pscale_ref, hprev_ref, wo_ref, g_ref, b_ref,
                         h1_ref, kx_ref, vx_ref, ux_ref, mix_ref, wos_ref, *, tm, n_j, n_tiles):
    t = pl.program_id(0)
    j = jnp.minimum(t, n_tiles - 1) % n_j
    slot = t % 2
    n_blk = tm // BLOCK

    @pl.when(t == 0)
    def _():
        wos_ref[...] = wo_ref[...]
        mix_ref[...] = jnp.zeros(mix_ref.shape, BF16)

    n_col = D_MODEL // V7X_MXU_COLS
    pieces = [(rb, nc) for rb in range(tm // ROW_SUB) for nc in range(n_col)]
    assert len(pieces) == n_blk * N_KV
    parts = []

    def outproj_piece(idx):
        rb, nc = pieces[idx]
        rows = slice(rb * ROW_SUB, (rb + 1) * ROW_SUB)
        cols = slice(nc * V7X_MXU_COLS, (nc + 1) * V7X_MXU_COLS)
        parts.append(jnp.dot(mix_ref[1 - slot, rows, :], wos_ref[:, cols], preferred_element_type=F32))
        if nc == n_col - 1:
            proj = jnp.concatenate(parts, axis=1)
            parts.clear()
            h1_ref[rows, :] = _layer_norm(ALPHA * hprev_ref[rows, :] + proj, g_ref[...], b_ref[...])

    lane_head = lax.broadcasted_iota(jnp.int32, (1, KV_COLS), 1) // HEAD_DIM
    pad = BLOCK - N_META

    @pl.when(j == 0)
    def _():
        zeros = jnp.zeros((pad, KV_COLS), BF16)
        kx_ref[0:pad, :] = zeros
        vx_ref[0:pad, :] = zeros
        kx_ref[pad:BLOCK, :] = kmeta_ref[...].astype(BF16)
        vx_ref[pad:BLOCK, :] = vmeta_ref[...].astype(BF16)
        ux_ref[0:N_META, :] = umeta_ref[...]

    @pl.when(j > 0)
    def _():
        kx_ref[0:BLOCK, :] = khalo_ref[...]
        vx_ref[0:BLOCK, :] = vhalo_ref[...]
        ux_ref[0:N_META, :] = uhalo_ref[...]

    kx_ref[BLOCK:, :] = k_ref[...]
    vx_ref[BLOCK:, :] = v_ref[...]
    ux_ref[N_META:, :] = u_ref[...]

    n_rows = GQA * BLOCK
    qi = lax.broadcasted_iota(jnp.int32, (n_rows, 2 * BLOCK), 0) % BLOCK
    ki = lax.broadcasted_iota(jnp.int32, (n_rows, 2 * BLOCK), 1)
    band = (ki >= qi) & (ki <= qi + WINDOW)
    row_group = lax.broadcasted_iota(jnp.int32, (n_rows, 1), 0) // BLOCK
    sink_cols = []
    for hh in range(N_KV):
        sink = jnp.zeros((n_rows, 1), F32)
        for gg in range(GQA):
            sink = jnp.where(row_group == gg, sinks_ref[hh * GQA + gg] * LOG2E, sink)
        sink_cols.append(sink)

    for blk in range(n_blk):
        r0 = blk * BLOCK
        rows = slice(r0, r0 + BLOCK)

        e = ux_ref[r0:r0 + N_META + BLOCK, :]
        sums = []
        cur = e
        for step in (1, 2, 4, 8):
            cur = cur + pltpu.roll(cur, step, 0)
            sums.append(cur[:, :POOL_GW])
            cur = cur[:, POOL_GW:]
        pooled = []
        for g, w in enumerate(POOL_WINDOWS):
            cs = slice(g * POOL_GW, (g + 1) * POOL_GW)
            d = sums[g][N_META:] * (1.0 / w) - e[N_META:, cs]
            pooled.append(_pool_group_matmul(d, g, wpool_ref, pscale_ref))

        if blk == 0:
            mask = band & (ki >= jnp.where(j == 0, pad, 0))
        else:
            mask = band
        qs = jnp.concatenate([q_ref[rows, gg * KV_COLS:(gg + 1) * KV_COLS] for gg in range(GQA)], axis=0)
        pns, vbs = [], []
        for hh in range(N_KV):
            kb = jnp.where(lane_head == hh, kx_ref[r0:r0 + 2 * BLOCK, :], 0)
            vbs.append(jnp.where(lane_head == hh, vx_ref[r0:r0 + 2 * BLOCK, :], 0))
            s = lax.dot_general(qs, kb, (((1,), (1,)), ((), ())), preferred_element_type=F32)
            s = jnp.where(mask, s, NEG_INF)
            sink = sink_cols[hh]
            m = jnp.max(s, axis=-1, keepdims=True)
            p = jnp.exp2(s - m)
            denom = jnp.sum(p, axis=-1, keepdims=True) + jnp.exp2(sink - m)
            pns.append((p * (1.0 / denom)).astype(BF16))
            outproj_piece(blk * N_KV + hh)
        o = jnp.dot(jnp.concatenate(pns, axis=1), jnp.concatenate(vbs, axis=0), preferred_element_type=F32)
        attn = _swap_head_order([o[gg * BLOCK:(gg + 1) * BLOCK, c * 128:(c + 1) * 128]
                                 for gg in range(GQA) for c in range(KV_COLS // 128)])
        mix_ref[slot, rows, :] = jnp.concatenate([a.astype(BF16) for a in attn] + pooled, axis=1)


def _prompt_mixer(sinks, q, k, v, u, h, kmeta, vmeta, umeta, wpool, pscale, wo, ln_g, ln_b, tm):
    bsz, seq = q.shape[0], q.shape[1]
    n_j = seq // tm
    n_tiles = bsz * n_j
    cur = lambda t: jnp.minimum(t, n_tiles - 1)
    tile = lambda t: (cur(t) // n_j, cur(t) % n_j, 0)
    fixed2 = lambda t: (0, 0)
    fixed3 = lambda t: (0, 0, 0)
    kv_halo = lambda t: (cur(t) // n_j, jnp.maximum((cur(t) % n_j) * (tm // BLOCK) - 1, 0), 0)
    u_halo = lambda t: (cur(t) // n_j, jnp.maximum((cur(t) % n_j) * (tm // N_META) - 1, 0), 0)
    prev = lambda t: (jnp.maximum(t - 1, 0), 0)
    return pl.pallas_call(
        functools.partial(_prompt_mixer_kernel, tm=tm, n_j=n_j, n_tiles=n_tiles),
        grid=(n_tiles + 1,),
        in_specs=[
            pl.BlockSpec(memory_space=pltpu.SMEM),
            pl.BlockSpec((None, tm, Q_COLS), tile),
            pl.BlockSpec((None, tm, KV_COLS), tile),
            pl.BlockSpec((None, BLOCK, KV_COLS), kv_halo),
            pl.BlockSpec((N_META, KV_COLS), fixed2),
            pl.BlockSpec((None, tm, KV_COLS), tile),
            pl.BlockSpec((None, BLOCK, KV_COLS), kv_halo),
            pl.BlockSpec((N_META, KV_COLS), fixed2),
            pl.BlockSpec((None, tm, POOL_W), tile),
            pl.BlockSpec((None, N_META, POOL_W), u_halo),
            pl.BlockSpec((N_META, POOL_W), fixed2),
            pl.BlockSpec((len(POOL_WINDOWS), POOL_GW, POOL_GW), fixed3),
            pl.BlockSpec((1, POOL_W), fixed2),
            pl.BlockSpec((tm, D_MODEL), prev),
            pl.BlockSpec((D_MODEL, D_MODEL), fixed2, pipeline_mode=pl.Buffered(1)),
            pl.BlockSpec((1, D_MODEL), fixed2),
            pl.BlockSpec((1, D_MODEL), fixed2),
        ],
        out_specs=pl.BlockSpec((tm, D_MODEL), prev),
        out_shape=jax.ShapeDtypeStruct((bsz * seq, D_MODEL), F32),
        scratch_shapes=[
            pltpu.VMEM((BLOCK + tm, KV_COLS), BF16),
            pltpu.VMEM((BLOCK + tm, KV_COLS), BF16),
            pltpu.VMEM((N_META + tm, POOL_W), F32),
            pltpu.VMEM((2, tm, D_MODEL), BF16),
            pltpu.VMEM((D_MODEL, D_MODEL), BF16),
        ],
        compiler_params=_params(("arbitrary",)),
        name="prompt_mixer",
    )(sinks, q, k, k, kmeta, v, v, vmeta, u, u, umeta, wpool, pscale, h, wo, ln_g, ln_b)


def _sample_attn_kernel(sinks_ref, q_ref, kn_ref, vn_ref, ck_ref, cv_ref,
                        attn_ref, nk_ref, nv_ref):
    bb = q_ref.shape[0]
    wk = ck_ref.shape[1]
    ck = ck_ref[...]
    cv = cv_ref[...]
    kn = kn_ref[...]
    vn = vn_ref[...]
    head_of_lane = lax.broadcasted_iota(jnp.int32, (KV_COLS, 128), 0) // HEAD_DIM
    out_lane = lax.broadcasted_iota(jnp.int32, (KV_COLS, 128), 1)
    seg = (head_of_lane == out_lane).astype(BF16)
    lane_t = lax.broadcasted_iota(jnp.int32, (128, KV_COLS), 0)
    head_t = lax.broadcasted_iota(jnp.int32, (128, KV_COLS), 1) // HEAD_DIM
    spread = (lane_t == head_t).astype(BF16)
    lane = lax.broadcasted_iota(jnp.int32, (1, 128), 1)

    o_cols = []
    for gg in range(GQA):
        qg = q_ref[:, gg * KV_COLS:(gg + 1) * KV_COLS].astype(F32)
        prod = (ck * qg[:, None, :]).astype(BF16).reshape(bb * wk, KV_COLS)
        s = jnp.dot(prod, seg, preferred_element_type=F32).reshape(bb, wk, 128)
        s_new = jnp.dot((kn * qg).astype(BF16), seg, preferred_element_type=F32)
        sink = jnp.zeros((1, 128), F32)
        for hh in range(N_KV):
            sink = jnp.where(lane == hh, sinks_ref[hh * GQA + gg] * LOG2E, sink)
        m = jnp.maximum(jnp.maximum(jnp.max(s, axis=1), s_new), sink)
        p = jnp.exp2(s - m[:, None, :])
        p_new = jnp.exp2(s_new - m)
        denom = jnp.sum(p, axis=1) + p_new + jnp.exp2(sink - m)
        r = 1.0 / denom
        pn = (p * r[:, None, :]).astype(BF16).reshape(bb * wk, 128)
        pn_new = (p_new * r).astype(BF16)
        pe = jnp.dot(pn, spread, preferred_element_type=F32).reshape(bb, wk, KV_COLS)
        pe_new = jnp.dot(pn_new, spread, preferred_element_type=F32)
        o = jnp.sum(pe * cv, axis=1) + pe_new * vn
        o_cols += [o[:, c * 128:(c + 1) * 128] for c in range(KV_COLS // 128)]
    for c, oc in enumerate(_swap_head_order(o_cols)):
        attn_ref[:, c * 128:(c + 1) * 128] = oc.astype(BF16)

    nk_ref[:, 0:wk - 1, :] = ck_ref[:, 1:wk, :]
    nk_ref[:, wk - 1:wk, :] = kn[:, None, :]
    nv_ref[:, 0:wk - 1, :] = cv_ref[:, 1:wk, :]
    nv_ref[:, wk - 1:wk, :] = vn[:, None, :]


def _sample_attn(sinks, q, kn, vn, ck, cv, bb):
    bsz, wk = ck.shape[0], ck.shape[1]
    row = lambda i: (i, 0)
    cache = lambda i: (i, 0, 0)
    return pl.pallas_call(
        _sample_attn_kernel,
        grid=(bsz // bb,),
        in_specs=[
            pl.BlockSpec(memory_space=pltpu.SMEM),
            pl.BlockSpec((bb, Q_COLS), row),
            pl.BlockSpec((bb, KV_COLS), row),
            pl.BlockSpec((bb, KV_COLS), row),
            pl.BlockSpec((bb, wk, KV_COLS), cache),
            pl.BlockSpec((bb, wk, KV_COLS), cache),
        ],
        out_specs=[
            pl.BlockSpec((bb, Q_COLS), row),
            pl.BlockSpec((bb, wk, KV_COLS), cache),
            pl.BlockSpec((bb, wk, KV_COLS), cache),
        ],
        out_shape=[
            jax.ShapeDtypeStruct((bsz, Q_COLS), BF16),
            jax.ShapeDtypeStruct((bsz, wk, KV_COLS), F32),
            jax.ShapeDtypeStruct((bsz, wk, KV_COLS), F32),
        ],
        compiler_params=_params(("parallel",)),
        name="sample_attn",
    )(sinks, q, kn, vn, ck, cv)


def _sample_mix_kernel(attn_ref, u_ref, st_ref, h_ref, wpool_ref, pscale_ref, wo_ref, g_ref, b_ref,
                       h1_ref, nst_ref, mix_ref):
    u = u_ref[...]
    mix_ref[:, 0:Q_COLS] = attn_ref[...]
    for g, w in enumerate(POOL_WINDOWS):
        cs = slice(g * POOL_GW, (g + 1) * POOL_GW)
        acc = u[:, cs]
        for t in range(POOL_HIST - (w - 1), POOL_HIST):
            acc = acc + st_ref[t, :, cs]
        d = acc * (1.0 / w) - u[:, cs]
        mix_ref[:, Q_COLS + g * POOL_GW:Q_COLS + (g + 1) * POOL_GW] = _pool_group_matmul(d, g, wpool_ref, pscale_ref)
    nst_ref[0:POOL_HIST - 1] = st_ref[1:POOL_HIST]
    nst_ref[POOL_HIST - 1] = u
    h1_ref[...] = _outproj_ln(mix_ref[...], h_ref[...], wo_ref, g_ref, b_ref)


def _sample_mix(attn, u, state, h, wpool, pscale, wo, ln_g, ln_b):
    bsz = attn.shape[0]
    return pl.pallas_call(
        _sample_mix_kernel,
        out_shape=[
            jax.ShapeDtypeStruct((bsz, D_MODEL), F32),
            jax.ShapeDtypeStruct((POOL_HIST, bsz, POOL_W), F32),
        ],
        scratch_shapes=[pltpu.VMEM((bsz, D_MODEL), BF16)],
        compiler_params=pltpu.CompilerParams(vmem_limit_bytes=V7X_VMEM_LIMIT),
        name="sample_mix",
    )(attn, u, state, h, wpool, pscale, wo, ln_g, ln_b)


def _ffn_kernel(h_ref, wg_ref, wu_ref, wd_ref, g_ref, b_ref, o_ref):
    f = pl.program_id(1)
    last = pl.num_programs(1) - 1
    tm = h_ref.shape[0]
    tf = wg_ref.shape[1]

    @pl.when(f == 0)
    def _():
        o_ref[...] = ALPHA * h_ref[...]

    def activation(hb, c):
        cs = slice(c * FFN_SUB, (c + 1) * FFN_SUB)
        gate = jnp.dot(hb, wg_ref[:, cs], preferred_element_type=F32)
        up = jnp.dot(hb, wu_ref[:, cs], preferred_element_type=F32)
        return (gate * (1.0 / (1.0 + jnp.exp(-gate))) * up).astype(BF16)

    @pl.when(f < last)
    def _():
        hb = h_ref[...].astype(BF16)
        part = None
        for c in range(tf // FFN_SUB):
            p = jnp.dot(activation(hb, c), wd_ref[c * FFN_SUB:(c + 1) * FFN_SUB, :], preferred_element_type=F32)
            part = p if part is None else part + p
        o_ref[...] += part

    @pl.when(f == last)
    def _():
        hb = h_ref[...].astype(BF16)
        a = jnp.concatenate([activation(hb, c) for c in range(tf // FFN_SUB)], axis=1)
        sub = min(tm, ROW_SUB)
        for r in range(tm // sub):
            rows = slice(r * sub, (r + 1) * sub)
            p = jnp.dot(a[rows], wd_ref[...], preferred_element_type=F32)
            o_ref[rows, :] = _layer_norm(o_ref[rows, :] + p, g_ref[...], b_ref[...])


def _ffn(h1, wg, wu, wd, ln_g, ln_b, tm, tf):
    rows = h1.shape[0]
    return pl.pallas_call(
        _ffn_kernel,
        grid=(rows // tm, D_FF // tf),
        in_specs=[
            pl.BlockSpec((tm, D_MODEL), lambda i, f: (i, 0)),
            pl.BlockSpec((D_MODEL, tf), lambda i, f: (0, f)),
            pl.BlockSpec((D_MODEL, tf), lambda i, f: (0, f)),
            pl.BlockSpec((tf, D_MODEL), lambda i, f: (f, 0)),
            pl.BlockSpec((1, D_MODEL), lambda i, f: (0, 0)),
            pl.BlockSpec((1, D_MODEL), lambda i, f: (0, 0)),
        ],
        out_specs=pl.BlockSpec((tm, D_MODEL), lambda i, f: (i, 0)),
        out_shape=jax.ShapeDtypeStruct((rows, D_MODEL), F32),
        compiler_params=_params(("parallel", "arbitrary")),
        name="ffn",
    )(h1, wg, wu, wd, ln_g, ln_b)


def _rope_tables(pos):
    half = ROT_DIM // 2
    inv_freq = ROPE_THETA ** (-np.arange(half, dtype=np.float64) * 2.0 / ROT_DIM)
    ang = np.asarray(pos, np.float64)[:, None] * inv_freq
    cos, sin = np.cos(ang), np.sin(ang)
    t = ang.shape[0]
    ones = np.ones((t, HEAD_DIM - ROT_DIM))
    zeros = np.zeros((t, HEAD_DIM - ROT_DIM))
    zh = np.zeros((t, half))
    c = np.concatenate([cos, cos, ones], -1)
    s_lo = np.concatenate([-sin, zh, zeros], -1)
    s_hi = np.concatenate([zh, sin, zeros], -1)
    return tuple(jnp.asarray(np.concatenate([a, a], -1), F32) for a in (c, s_lo, s_hi))


def kernel(x_prompt, x_sample, cache_k, cache_v, state_pool, meta_tokens, ln_in_g, ln_in_b, w_in, b_in,
           attn_sinks, w_pool, pool_scale, w_o, ln1_g, ln1_b, w_gate, w_up, w_down, ln2_g, ln2_b):
    bp, seq, _ = x_prompt.shape
    bs = x_sample.shape[0]
    wk = cache_k.shape[2]
    assert x_sample.shape[1] == 1 and cache_k.shape[0] == 1

    w_in_p = w_in[0].astype(BF16)
    b_in_p = b_in[0][None]
    wpool = w_pool[0].astype(BF16)
    pscale = pool_scale[0][None]
    sinks = attn_sinks[0]
    g_in, b_ln_in = ln_in_g[None], ln_in_b[None]
    g1, b1, g2, b2 = ln1_g[0][None], ln1_b[0][None], ln2_g[0][None], ln2_b[0][None]

    tabs_prompt = _rope_tables(N_META + np.arange(seq))
    tabs_small = _rope_tables(np.concatenate([np.arange(N_META), np.full((bs,), PAST_LEN)]))

    x_small = jnp.concatenate([meta_tokens, x_sample.reshape(bs, D_MODEL)], 0)
    h_ms, q_ms, k_ms, v_ms, u_ms = _inproj(x_small, g_in, b_ln_in, w_in_p, b_in_p, tabs_small, N_META + bs)
    k_meta, v_meta, u_meta = k_ms[:N_META], v_ms[:N_META], u_ms[:N_META]
    h_s, q_s, k_s, v_s, u_s = h_ms[N_META:], q_ms[N_META:], k_ms[N_META:], v_ms[N_META:], u_ms[N_META:]

    h_p, q_p, k_p, v_p, u_p, wg, wu, wd, w_o_p, k_tail, v_tail, u_tail = _inproj(
        x_prompt.reshape(bp * seq, D_MODEL), g_in, b_ln_in, w_in_p, b_in_p, tabs_prompt, TM_INPROJ,
        cast=(w_gate[0], w_up[0], w_down[0], w_o[0]), seq_steps=seq // TM_INPROJ)
    k_p3 = k_p.reshape(bp, seq, KV_COLS)
    v_p3 = v_p.reshape(bp, seq, KV_COLS)
    u_p3 = u_p.reshape(bp, seq, POOL_W)
    h1_p = _prompt_mixer(sinks, q_p.reshape(bp, seq, Q_COLS), k_p3, v_p3, u_p3, h_p,
                         k_meta, v_meta, u_meta, wpool, pscale, w_o_p, g1, b1, TM_MIXER)
    y_prompt = _ffn(h1_p, wg, wu, wd, g2, b2, TM_FFN, TF_FFN).reshape(bp, seq, D_MODEL)

    attn_s, nk_s, nv_s = _sample_attn(sinks, q_s, k_s, v_s, cache_k.reshape(bs, wk, KV_COLS),
                                      cache_v.reshape(bs, wk, KV_COLS), BB_SAMPLE)
    h1_s, npool_s = _sample_mix(attn_s, u_s, state_pool[0].swapaxes(0, 1), h_s, wpool, pscale, w_o_p, g1, b1)
    y_sample = _ffn(h1_s, wg, wu, wd, g2, b2, bs, TF_FFN).reshape(bs, 1, D_MODEL)

    new_k_prompt = k_tail.reshape(1, bp, WINDOW, N_KV, HEAD_DIM)
    new_v_prompt = v_tail.reshape(1, bp, WINDOW, N_KV, HEAD_DIM)
    new_pool_prompt = u_tail[:, N_META - POOL_HIST:][None]
    return (y_prompt, y_sample, new_k_prompt, new_v_prompt, new_pool_prompt,
            nk_s.reshape(1, bs, wk, N_KV, HEAD_DIM), nv_s.reshape(1, bs, wk, N_KV, HEAD_DIM), npool_s.swapaxes(0, 1)[None])
```

```python
import functools
import math

import jax
import numpy as np
import jax.numpy as jnp
from jax import lax
from jax.experimental import pallas as pl
from jax.experimental.pallas import tpu as pltpu

D_MODEL = 2048
N_META = 16
HEAD_DIM = 64
N_KV = 4
GQA = 4
Q_COLS = N_KV * GQA * HEAD_DIM
KV_COLS = N_KV * HEAD_DIM
POOL_W = 1024
POOL_WINDOWS = (2, 4, 8, 16)
POOL_GW = POOL_W // len(POOL_WINDOWS)
POOL_HIST = 15
IN_COLS = Q_COLS + 2 * KV_COLS + POOL_W
D_FF = 5632
WINDOW = 128
BLOCK = 128
ROT_DIM = 16
ROPE_THETA = 500000.0
PAST_LEN = 16384
ALPHA = 2.0 ** 0.25
LN_EPS = 1e-5
NEG_INF = -1e30
LOG2E = math.log2(math.e)
QK_SCALE = LOG2E / math.sqrt(HEAD_DIM)
V7X_MXU_COLS = 256
FFN_SUB = V7X_MXU_COLS
ROW_SUB = 256

V7X_VMEM_LIMIT = 62 * 1024 * 1024

TM_INPROJ = 512
TM_MIXER = 512
TM_FFN = 1024
TF_FFN = 512
BB_SAMPLE = 32

BF16 = jnp.bfloat16
F32 = jnp.float32


def _layer_norm(x, g, b):
    mu = jnp.mean(x, axis=-1, keepdims=True)
    xc = x - mu
    var = jnp.mean(xc * xc, axis=-1, keepdims=True)
    return xc * lax.rsqrt(var + LN_EPS) * g + b


def _params(sem):
    return pltpu.CompilerParams(dimension_semantics=sem, vmem_limit_bytes=V7X_VMEM_LIMIT)


def _cast_specs(cast, steps):
    specs = []
    for a in cast:
        span = 1
        while a.shape[0] % (steps // span) or (a.shape[0] // (steps // span)) % 16:
            span *= 2
        specs.append(pl.BlockSpec((a.shape[0] // (steps // span), a.shape[1]),
                                  functools.partial(lambda i, sp: (i // sp, 0), sp=span)))
    return specs


def _cast_slabs(cast_in, cast_out):
    for src_ref, dst_ref in zip(cast_in, cast_out):
        dst_ref[...] = src_ref[...].astype(BF16)


def _swap_head_order(cols):
    low = lax.broadcasted_iota(jnp.int32, (1, 128), 1) < HEAD_DIM
    out = []
    for j in range(2 * N_KV):
        b, a0 = j // 2, 2 * (j % 2)
        x = cols[(a0 * 4 + b) // 2]
        y = cols[((a0 + 1) * 4 + b) // 2]
        if b % 2 == 0:
            out.append(jnp.where(low, x, pltpu.roll(y, HEAD_DIM, 1)))
        else:
            out.append(jnp.where(low, pltpu.roll(x, HEAD_DIM, 1), y))
    return out


def _inproj_kernel(x_ref, g_ref, b_ref, w_ref, bias_ref, c_ref, s1_ref, s2_ref, *rest, n_cast):
    cast_in, (h_ref, q_ref, k_ref, v_ref, u_ref) = rest[:n_cast], rest[n_cast:n_cast + 5]
    cast_out, tails = rest[n_cast + 5:2 * n_cast + 5], rest[2 * n_cast + 5:]
    _cast_slabs(cast_in, cast_out)
    tm = x_ref.shape[0]
    sub = min(tm, ROW_SUB)
    for r in range(tm // sub):
        rows = slice(r * sub, (r + 1) * sub)
        h = _layer_norm(x_ref[rows, :], g_ref[...], b_ref[...])
        h_ref[rows, :] = h
        z = jnp.dot(h.astype(BF16), w_ref[...], preferred_element_type=F32) + bias_ref[...]
        cos = c_ref[rows, :]
        s_lo = s1_ref[rows, :]
        s_hi = s2_ref[rows, :]

        def rope(zc):
            return (zc * cos + pltpu.roll(zc, 128 - ROT_DIM // 2, 1) * s_lo
                    + pltpu.roll(zc, ROT_DIM // 2, 1) * s_hi)

        q_cols = _swap_head_order([rope(z[:, c * 128:(c + 1) * 128]) for c in range(Q_COLS // 128)])
        for c, qc in enumerate(q_cols):
            q_ref[rows, c * 128:(c + 1) * 128] = (qc * QK_SCALE).astype(BF16)
        k = jnp.concatenate([rope(z[:, Q_COLS + c * 128:Q_COLS + (c + 1) * 128]) for c in range(KV_COLS // 128)], 1)
        v = z[:, Q_COLS + KV_COLS:Q_COLS + 2 * KV_COLS]
        u = z[:, Q_COLS + 2 * KV_COLS:]
        k_ref[rows, :] = k.astype(k_ref.dtype)
        v_ref[rows, :] = v.astype(v_ref.dtype)
        u_ref[rows, :] = u
        if tails and r == tm // sub - 1:
            k_tail, v_tail, u_tail = tails
            k_tail[...] = k[sub - WINDOW:]
            v_tail[...] = v[sub - WINDOW:]
            u_tail[...] = u[sub - N_META:]


def _inproj(x, ln_g, ln_b, w_in, b_in, tabs, tm, cast=(), seq_steps=None):
    rows = x.shape[0]
    steps = rows // tm
    kv_dtype = F32 if seq_steps is None else BF16
    tail_specs, tail_shapes = [], []
    if seq_steps is not None:
        n_seq = steps // seq_steps
        seq_of = lambda i: (i // seq_steps, 0, 0)
        for n_rows, width in ((WINDOW, KV_COLS), (WINDOW, KV_COLS), (N_META, POOL_W)):
            tail_specs.append(pl.BlockSpec((None, n_rows, width), seq_of))
            tail_shapes.append(jax.ShapeDtypeStruct((n_seq, n_rows, width), F32))
    t_rows = tabs[0].shape[0]
    n_t = t_rows // tm
    row = lambda i: (i, 0)
    fixed = lambda i: (0, 0)
    tab = lambda i: (i % n_t, 0)
    cast_specs = _cast_specs(cast, steps)
    return pl.pallas_call(
        functools.partial(_inproj_kernel, n_cast=len(cast)),
        grid=(steps,),
        in_specs=[
            pl.BlockSpec((tm, D_MODEL), row),
            pl.BlockSpec((1, D_MODEL), fixed),
            pl.BlockSpec((1, D_MODEL), fixed),
            pl.BlockSpec((D_MODEL, IN_COLS), fixed),
            pl.BlockSpec((1, IN_COLS), fixed),
            pl.BlockSpec((tm, 128), tab),
            pl.BlockSpec((tm, 128), tab),
            pl.BlockSpec((tm, 128), tab),
        ] + cast_specs,
        out_specs=[
            pl.BlockSpec((tm, D_MODEL), row),
            pl.BlockSpec((tm, Q_COLS), row),
            pl.BlockSpec((tm, KV_COLS), row),
            pl.BlockSpec((tm, KV_COLS), row),
            pl.BlockSpec((tm, POOL_W), row),
        ] + cast_specs + tail_specs,
        out_shape=[
            jax.ShapeDtypeStruct((rows, D_MODEL), F32),
            jax.ShapeDtypeStruct((rows, Q_COLS), BF16),
            jax.ShapeDtypeStruct((rows, KV_COLS), kv_dtype),
            jax.ShapeDtypeStruct((rows, KV_COLS), kv_dtype),
            jax.ShapeDtypeStruct((rows, POOL_W), F32),
        ] + [jax.ShapeDtypeStruct(a.shape, BF16) for a in cast] + tail_shapes,
        compiler_params=_params(("arbitrary",)),
        name="inproj",
    )(x, ln_g, ln_b, w_in, b_in, *tabs, *cast)


def _outproj_ln(mix_bf16, h, wo_ref, g_ref, b_ref):
    mix = jnp.dot(mix_bf16, wo_ref[...], preferred_element_type=F32)
    return _layer_norm(ALPHA * h + mix, g_ref[...], b_ref[...])


def _pool_group_matmul(d, g, wpool_ref, pscale_ref):
    cs = slice(g * POOL_GW, (g + 1) * POOL_GW)
    y = jnp.dot(d.astype(BF16), wpool_ref[g], preferred_element_type=F32)
    return (y * pscale_ref[:, cs]).astype(BF16)


def _prompt_mixer_kernel(sinks_ref, q_ref, k_ref, khalo_ref, kmeta_ref, v_ref, vhalo_ref, vmeta_ref,
                         u_ref, uhalo_ref, umeta_ref, wpool_ref, pscale_ref, hprev_ref, wo_ref, g_ref, b_ref,
                         h1_ref, mix_ref, wos_ref, *, tm, n_j, n_tiles):
    t = pl.program_id(0)
    j = jnp.minimum(t, n_tiles - 1) % n_j
    slot = t % 2
    n_blk = tm // BLOCK

    @pl.when(t == 0)
    def _():
        wos_ref[...] = wo_ref[...]
        mix_ref[...] = jnp.zeros(mix_ref.shape, BF16)

    n_col = D_MODEL // V7X_MXU_COLS
    pieces = [(rb, nc) for rb in range(tm // ROW_SUB) for nc in range(n_col)]
    assert len(pieces) == n_blk * N_KV
    parts = []

    def outproj_piece(idx):
        rb, nc = pieces[idx]
        rows = slice(rb * ROW_SUB, (rb + 1) * ROW_SUB)
        cols = slice(nc * V7X_MXU_COLS, (nc + 1) * V7X_MXU_COLS)
        parts.append(jnp.dot(mix_ref[1 - slot, rows, :], wos_ref[:, cols], preferred_element_type=F32))
        if nc == n_col - 1:
            proj = jnp.concatenate(parts, axis=1)
            parts.clear()
            h1_ref[rows, :] = _layer_norm(ALPHA * hprev_ref[rows, :] + proj, g_ref[...], b_ref[...])

    lane_head = lax.broadcasted_iota(jnp.int32, (1, KV_COLS), 1) // HEAD_DIM
    pad = BLOCK - N_META

    first = j == 0
    zeros = jnp.zeros((pad, KV_COLS), BF16)
    k_prev = jnp.where(first, jnp.concatenate([zeros, kmeta_ref[...].astype(BF16)], axis=0), khalo_ref[...])
    v_prev = jnp.where(first, jnp.concatenate([zeros, vmeta_ref[...].astype(BF16)], axis=0), vhalo_ref[...])
    u_prev = jnp.where(first, umeta_ref[...], uhalo_ref[...])

    n_rows = GQA * BLOCK
    qi = lax.broadcasted_iota(jnp.int32, (n_rows, 2 * BLOCK), 0) % BLOCK
    ki = lax.broadcasted_iota(jnp.int32, (n_rows, 2 * BLOCK), 1)
    band = (ki >= qi) & (ki <= qi + WINDOW)
    row_group = lax.broadcasted_iota(jnp.int32, (n_rows, 1), 0) // BLOCK
    sink_cols = []
    for hh in range(N_KV):
        sink = jnp.zeros((n_rows, 1), F32)
        for gg in range(GQA):
            sink = jnp.where(row_group == gg, sinks_ref[hh * GQA + gg] * LOG2E, sink)
        sink_cols.append(sink)

    for blk in range(n_blk):
        r0 = blk * BLOCK
        rows = slice(r0, r0 + BLOCK)

        if blk == 0:
            e = jnp.concatenate([u_prev, u_ref[0:BLOCK, :]], axis=0)
            k_band = jnp.concatenate([k_prev, k_ref[0:BLOCK, :]], axis=0)
            v_band = jnp.concatenate([v_prev, v_ref[0:BLOCK, :]], axis=0)
        else:
            e = u_ref[r0 - N_META:r0 + BLOCK, :]
            k_band = k_ref[r0 - BLOCK:r0 + BLOCK, :]
            v_band = v_ref[r0 - BLOCK:r0 + BLOCK, :]

        sums = []
        cur = e
        for step in (1, 2, 4, 8):
            cur = cur + pltpu.roll(cur, step, 0)
            sums.append(cur[:, :POOL_GW])
            cur = cur[:, POOL_GW:]
        pooled = []
        for g, w in enumerate(POOL_WINDOWS):
            cs = slice(g * POOL_GW, (g + 1) * POOL_GW)
            d = sums[g][N_META:] * (1.0 / w) - e[N_META:, cs]
            pooled.append(_pool_group_matmul(d, g, wpool_ref, pscale_ref))

        if blk == 0:
            mask = band & (ki >= jnp.where(j == 0, pad, 0))
        else:
            mask = band
        qs = jnp.concatenate([q_ref[rows, gg * KV_COLS:(gg + 1) * KV_COLS] for gg in range(GQA)], axis=0)
        pns, vbs = [], []
        for hh in range(N_KV):
            kb = jnp.where(lane_head == hh, k_band, 0)
            vbs.append(jnp.where(lane_head == hh, v_band, 0))
            s = lax.dot_general(qs, kb, (((1,), (1,)), ((), ())), preferred_element_type=F32)
            s = jnp.where(mask, s, NEG_INF)
            sink = sink_cols[hh]
            m = jnp.max(s, axis=-1, keepdims=True)
            p = jnp.exp2(s - m)
            denom = jnp.sum(p, axis=-1, keepdims=True) + jnp.exp2(sink - m)
            pns.append((p * (1.0 / denom)).astype(BF16))
            outproj_piece(blk * N_KV + hh)
        o = jnp.dot(jnp.concatenate(pns, axis=1), jnp.concatenate(vbs, axis=0), preferred_element_type=F32)
        attn = _swap_head_order([o[gg * BLOCK:(gg + 1) * BLOCK, c * 128:(c + 1) * 128]
                                 for gg in range(GQA) for c in range(KV_COLS // 128)])
        mix_ref[slot, rows, :] = jnp.concatenate([a.astype(BF16) for a in attn] + pooled, axis=1)


def _prompt_mixer(sinks, q, k, v, u, h, kmeta, vmeta, umeta, wpool, pscale, wo, ln_g, ln_b, tm):
    bsz, seq = q.shape[0], q.shape[1]
    n_j = seq // tm
    n_tiles = bsz * n_j
    cur = lambda t: jnp.minimum(t, n_tiles - 1)
    tile = lambda t: (cur(t) // n_j, cur(t) % n_j, 0)
    fixed2 = lambda t: (0, 0)
    fixed3 = lambda t: (0, 0, 0)
    kv_halo = lambda t: (cur(t) // n_j, jnp.maximum((cur(t) % n_j) * (tm // BLOCK) - 1, 0), 0)
    u_halo = lambda t: (cur(t) // n_j, jnp.maximum((cur(t) % n_j) * (tm // N_META) - 1, 0), 0)
    prev = lambda t: (jnp.maximum(t - 1, 0), 0)
    return pl.pallas_call(
        functools.partial(_prompt_mixer_kernel, tm=tm, n_j=n_j, n_tiles=n_tiles),
        grid=(n_tiles + 1,),
        in_specs=[
            pl.BlockSpec(memory_space=pltpu.SMEM),
            pl.BlockSpec((None, tm, Q_COLS), tile),
            pl.BlockSpec((None, tm, KV_COLS), tile),
            pl.BlockSpec((None, BLOCK, KV_COLS), kv_halo),
            pl.BlockSpec((N_META, KV_COLS), fixed2),
            pl.BlockSpec((None, tm, KV_COLS), tile),
            pl.BlockSpec((None, BLOCK, KV_COLS), kv_halo),
            pl.BlockSpec((N_META, KV_COLS), fixed2),
            pl.BlockSpec((None, tm, POOL_W), tile),
            pl.BlockSpec((None, N_META, POOL_W), u_halo),
            pl.BlockSpec((N_META, POOL_W), fixed2),
            pl.BlockSpec((len(POOL_WINDOWS), POOL_GW, POOL_GW), fixed3),
            pl.BlockSpec((1, POOL_W), fixed2),
            pl.BlockSpec((tm, D_MODEL), prev),
            pl.BlockSpec((D_MODEL, D_MODEL), fixed2, pipeline_mode=pl.Buffered(1)),
            pl.BlockSpec((1, D_MODEL), fixed2),
            pl.BlockSpec((1, D_MODEL), fixed2),
        ],
        out_specs=pl.BlockSpec((tm, D_MODEL), prev),
        out_shape=jax.ShapeDtypeStruct((bsz * seq, D_MODEL), F32),
        scratch_shapes=[
            pltpu.VMEM((2, tm, D_MODEL), BF16),
            pltpu.VMEM((D_MODEL, D_MODEL), BF16),
        ],
        compiler_params=_params(("arbitrary",)),
        name="prompt_mixer",
    )(sinks, q, k, k, kmeta, v, v, vmeta, u, u, umeta, wpool, pscale, h, wo, ln_g, ln_b)


def _sample_attn_kernel(sinks_ref, q_ref, kn_ref, vn_ref, ck_ref, cv_ref,
                        attn_ref, nk_ref, nv_ref):
    bb = q_ref.shape[0]
    wk = ck_ref.shape[1]
    ck = ck_ref[...]
    cv = cv_ref[...]
    kn = kn_ref[...]
    vn = vn_ref[...]
    head_of_lane = lax.broadcasted_iota(jnp.int32, (KV_COLS, 128), 0) // HEAD_DIM
    out_lane = lax.broadcasted_iota(jnp.int32, (KV_COLS, 128), 1)
    seg = (head_of_lane == out_lane).astype(BF16)
    lane_t = lax.broadcasted_iota(jnp.int32, (128, KV_COLS), 0)
    head_t = lax.broadcasted_iota(jnp.int32, (128, KV_COLS), 1) // HEAD_DIM
    spread = (lane_t == head_t).astype(BF16)
    lane = lax.broadcasted_iota(jnp.int32, (1, 128), 1)

    o_cols = []
    for gg in range(GQA):
        qg = q_ref[:, gg * KV_COLS:(gg + 1) * KV_COLS].astype(F32)
        prod = (ck * qg[:, None, :]).astype(BF16).reshape(bb * wk, KV_COLS)
        s = jnp.dot(prod, seg, preferred_element_type=F32).reshape(bb, wk, 128)
        s_new = jnp.dot((kn * qg).astype(BF16), seg, preferred_element_type=F32)
        sink = jnp.zeros((1, 128), F32)
        for hh in range(N_KV):
            sink = jnp.where(lane == hh, sinks_ref[hh * GQA + gg] * LOG2E, sink)
        m = jnp.maximum(jnp.maximum(jnp.max(s, axis=1), s_new), sink)
        p = jnp.exp2(s - m[:, None, :])
        p_new = jnp.exp2(s_new - m)
        denom = jnp.sum(p, axis=1) + p_new + jnp.exp2(sink - m)
        r = 1.0 / denom
        pn = (p * r[:, None, :]).astype(BF16).reshape(bb * wk, 128)
        pn_new = (p_new * r).astype(BF16)
        pe = jnp.dot(pn, spread, preferred_element_type=F32).reshape(bb, wk, KV_COLS)
        pe_new = jnp.dot(pn_new, spread, preferred_element_type=F32)
        o = jnp.sum(pe * cv, axis=1) + pe_new * vn
        o_cols += [o[:, c * 128:(c + 1) * 128] for c in range(KV_COLS // 128)]
    for c, oc in enumerate(_swap_head_order(o_cols)):
        attn_ref[:, c * 128:(c + 1) * 128] = oc.astype(BF16)

    nk_ref[:, 0:wk - 1, :] = ck_ref[:, 1:wk, :]
    nk_ref[:, wk - 1:wk, :] = kn[:, None, :]
    nv_ref[:, 0:wk - 1, :] = cv_ref[:, 1:wk, :]
    nv_ref[:, wk - 1:wk, :] = vn[:, None, :]


def _sample_attn(sinks, q, kn, vn, ck, cv, bb):
    bsz, wk = ck.shape[0], ck.shape[1]
    row = lambda i: (i, 0)
    cache = lambda i: (i, 0, 0)
    return pl.pallas_call(
        _sample_attn_kernel,
        grid=(bsz // bb,),
        in_specs=[
            pl.BlockSpec(memory_space=pltpu.SMEM),
            pl.BlockSpec((bb, Q_COLS), row),
            pl.BlockSpec((bb, KV_COLS), row),
            pl.BlockSpec((bb, KV_COLS), row),
            pl.BlockSpec((bb, wk, KV_COLS), cache),
            pl.BlockSpec((bb, wk, KV_COLS), cache),
        ],
        out_specs=[
            pl.BlockSpec((bb, Q_COLS), row),
            pl.BlockSpec((bb, wk, KV_COLS), cache),
            pl.BlockSpec((bb, wk, KV_COLS), cache),
        ],
        out_shape=[
            jax.ShapeDtypeStruct((bsz, Q_COLS), BF16),
            jax.ShapeDtypeStruct((bsz, wk, KV_COLS), F32),
            jax.ShapeDtypeStruct((bsz, wk, KV_COLS), F32),
        ],
        compiler_params=_params(("parallel",)),
        name="sample_attn",
    )(sinks, q, kn, vn, ck, cv)


def _sample_mix_kernel(attn_ref, u_ref, st_ref, h_ref, wpool_ref, pscale_ref, wo_ref, g_ref, b_ref,
                       h1_ref, nst_ref, mix_ref):
    u = u_ref[...]
    mix_ref[:, 0:Q_COLS] = attn_ref[...]
    for g, w in enumerate(POOL_WINDOWS):
        cs = slice(g * POOL_GW, (g + 1) * POOL_GW)
        acc = u[:, cs]
        for t in range(POOL_HIST - (w - 1), POOL_HIST):
            acc = acc + st_ref[t, :, cs]
        d = acc * (1.0 / w) - u[:, cs]
        mix_ref[:, Q_COLS + g * POOL_GW:Q_COLS + (g + 1) * POOL_GW] = _pool_group_matmul(d, g, wpool_ref, pscale_ref)
    nst_ref[0:POOL_HIST - 1] = st_ref[1:POOL_HIST]
    nst_ref[POOL_HIST - 1] = u
    h1_ref[...] = _outproj_ln(mix_ref[...], h_ref[...], wo_ref, g_ref, b_ref)


def _sample_mix(attn, u, state, h, wpool, pscale, wo, ln_g, ln_b):
    bsz = attn.shape[0]
    return pl.pallas_call(
        _sample_mix_kernel,
        out_shape=[
            jax.ShapeDtypeStruct((bsz, D_MODEL), F32),
            jax.ShapeDtypeStruct((POOL_HIST, bsz, POOL_W), F32),
        ],
        scratch_shapes=[pltpu.VMEM((bsz, D_MODEL), BF16)],
        compiler_params=pltpu.CompilerParams(vmem_limit_bytes=V7X_VMEM_LIMIT),
        name="sample_mix",
    )(attn, u, state, h, wpool, pscale, wo, ln_g, ln_b)


def _ffn_kernel(h_ref, wg_ref, wu_ref, wd_ref, g_ref, b_ref, o_ref):
    f = pl.program_id(1)
    last = pl.num_programs(1) - 1
    tm = h_ref.shape[0]
    tf = wg_ref.shape[1]

    @pl.when(f == 0)
    def _():
        o_ref[...] = ALPHA * h_ref[...]

    def activation(hb, c):
        cs = slice(c * FFN_SUB, (c + 1) * FFN_SUB)
        gate = jnp.dot(hb, wg_ref[:, cs], preferred_element_type=F32)
        up = jnp.dot(hb, wu_ref[:, cs], preferred_element_type=F32)
        return (gate * (1.0 / (1.0 + jnp.exp(-gate))) * up).astype(BF16)

    @pl.when(f < last)
    def _():
        hb = h_ref[...].astype(BF16)
        part = None
        for c in range(tf // FFN_SUB):
            p = jnp.dot(activation(hb, c), wd_ref[c * FFN_SUB:(c + 1) * FFN_SUB, :], preferred_element_type=F32)
            part = p if part is None else part + p
        o_ref[...] += part

    @pl.when(f == last)
    def _():
        hb = h_ref[...].astype(BF16)
        a = jnp.concatenate([activation(hb, c) for c in range(tf // FFN_SUB)], axis=1)
        sub = min(tm, ROW_SUB)
        for r in range(tm // sub):
            rows = slice(r * sub, (r + 1) * sub)
            p = jnp.dot(a[rows], wd_ref[...], preferred_element_type=F32)
            o_ref[rows, :] = _layer_norm(o_ref[rows, :] + p, g_ref[...], b_ref[...])


def _ffn(h1, wg, wu, wd, ln_g, ln_b, tm, tf):
    rows = h1.shape[0]
    return pl.pallas_call(
        _ffn_kernel,
        grid=(rows // tm, D_FF // tf),
        in_specs=[
            pl.BlockSpec((tm, D_MODEL), lambda i, f: (i, 0)),
            pl.BlockSpec((D_MODEL, tf), lambda i, f: (0, f)),
            pl.BlockSpec((D_MODEL, tf), lambda i, f: (0, f)),
            pl.BlockSpec((tf, D_MODEL), lambda i, f: (f, 0)),
            pl.BlockSpec((1, D_MODEL), lambda i, f: (0, 0)),
            pl.BlockSpec((1, D_MODEL), lambda i, f: (0, 0)),
        ],
        out_specs=pl.BlockSpec((tm, D_MODEL), lambda i, f: (i, 0)),
        out_shape=jax.ShapeDtypeStruct((rows, D_MODEL), F32),
        compiler_params=_params(("parallel", "arbitrary")),
        name="ffn",
    )(h1, wg, wu, wd, ln_g, ln_b)


def _rope_tables(pos):
    half = ROT_DIM // 2
    inv_freq = ROPE_THETA ** (-np.arange(half, dtype=np.float64) * 2.0 / ROT_DIM)
    ang = np.asarray(pos, np.float64)[:, None] * inv_freq
    cos, sin = np.cos(ang), np.sin(ang)
    t = ang.shape[0]
    ones = np.ones((t, HEAD_DIM - ROT_DIM))
    zeros = np.zeros((t, HEAD_DIM - ROT_DIM))
    zh = np.zeros((t, half))
    c = np.concatenate([cos, cos, ones], -1)
    s_lo = np.concatenate([-sin, zh, zeros], -1)
    s_hi = np.concatenate([zh, sin, zeros], -1)
    return tuple(jnp.asarray(np.concatenate([a, a], -1), F32) for a in (c, s_lo, s_hi))


def kernel(x_prompt, x_sample, cache_k, cache_v, state_pool, meta_tokens, ln_in_g, ln_in_b, w_in, b_in,
           attn_sinks, w_pool, pool_scale, w_o, ln1_g, ln1_b, w_gate, w_up, w_down, ln2_g, ln2_b):
    bp, seq, _ = x_prompt.shape
    bs = x_sample.shape[0]
    wk = cache_k.shape[2]
    assert x_sample.shape[1] == 1 and cache_k.shape[0] == 1

    w_in_p = w_in[0].astype(BF16)
    b_in_p = b_in[0][None]
    wpool = w_pool[0].astype(BF16)
    pscale = pool_scale[0][None]
    sinks = attn_sinks[0]
    g_in, b_ln_in = ln_in_g[None], ln_in_b[None]
    g1, b1, g2, b2 = ln1_g[0][None], ln1_b[0][None], ln2_g[0][None], ln2_b[0][None]

    tabs_prompt = _rope_tables(N_META + np.arange(seq))
    tabs_small = _rope_tables(np.concatenate([np.arange(N_META), np.full((bs,), PAST_LEN)]))

    x_small = jnp.concatenate([meta_tokens, x_sample.reshape(bs, D_MODEL)], 0)
    h_ms, q_ms, k_ms, v_ms, u_ms = _inproj(x_small, g_in, b_ln_in, w_in_p, b_in_p, tabs_small, N_META + bs)
    k_meta, v_meta, u_meta = k_ms[:N_META], v_ms[:N_META], u_ms[:N_META]
    h_s, q_s, k_s, v_s, u_s = h_ms[N_META:], q_ms[N_META:], k_ms[N_META:], v_ms[N_META:], u_ms[N_META:]

    h_p, q_p, k_p, v_p, u_p, wg, wu, wd, w_o_p, k_tail, v_tail, u_tail = _inproj(
        x_prompt.reshape(bp * seq, D_MODEL), g_in, b_ln_in, w_in_p, b_in_p, tabs_prompt, TM_INPROJ,
        cast=(w_gate[0], w_up[0], w_down[0], w_o[0]), seq_steps=seq // TM_INPROJ)
    k_p3 = k_p.reshape(bp, seq, KV_COLS)
    v_p3 = v_p.reshape(bp, seq, KV_COLS)
    u_p3 = u_p.reshape(bp, seq, POOL_W)
    h1_p = _prompt_mixer(sinks, q_p.reshape(bp, seq, Q_COLS), k_p3, v_p3, u_p3, h_p,
                         k_meta, v_meta, u_meta, wpool, pscale, w_o_p, g1, b1, TM_MIXER)
    y_prompt = _ffn(h1_p, wg, wu, wd, g2, b2, TM_FFN, TF_FFN).reshape(bp, seq, D_MODEL)

    attn_s, nk_s, nv_s = _sample_attn(sinks, q_s, k_s, v_s, cache_k.reshape(bs, wk, KV_COLS),
                                      cache_v.reshape(bs, wk, KV_COLS), BB_SAMPLE)
    h1_s, npool_s = _sample_mix(attn_s, u_s, state_pool[0].swapaxes(0, 1), h_s, wpool, pscale, w_o_p, g1, b1)
    y_sample = _ffn(h1_s, wg, wu, wd, g2, b2, bs, TF_FFN).reshape(bs, 1, D_MODEL)

    new_k_prompt = k_tail.reshape(1, bp, WINDOW, N_KV, HEAD_DIM)
    new_v_prompt = v_tail.reshape(1, bp, WINDOW, N_KV, HEAD_DIM)
    new_pool_prompt = u_tail[:, N_META - POOL_HIST:][None]
    return (y_prompt, y_sample, new_k_prompt, new_v_prompt, new_pool_prompt,
            nk_s.reshape(1, bs, wk, N_KV, HEAD_DIM), nv_s.reshape(1, bs, wk, N_KV, HEAD_DIM), npool_s.swapaxes(0, 1)[None])
```

```python
import functools
import math

import jax
import numpy as np
import jax.numpy as jnp
from jax import lax
from jax.experimental import pallas as pl
from jax.experimental.pallas import tpu as pltpu

D_MODEL = 2048
N_META = 16
HEAD_DIM = 64
N_KV = 4
GQA = 4
Q_COLS = N_KV * GQA * HEAD_DIM
KV_COLS = N_KV * HEAD_DIM
POOL_W = 1024
POOL_WINDOWS = (2, 4, 8, 16)
POOL_GW = POOL_W // len(POOL_WINDOWS)
POOL_HIST = 15
IN_COLS = Q_COLS + 2 * KV_COLS + POOL_W
D_FF = 5632
WINDOW = 128
BLOCK = 128
ROT_DIM = 16
ROPE_THETA = 500000.0
PAST_LEN = 16384
ALPHA = 2.0 ** 0.25
LN_EPS = 1e-5
NEG_INF = -1e30
LOG2E = math.log2(math.e)
QK_SCALE = LOG2E / math.sqrt(HEAD_DIM)
V7X_MXU_COLS = 256
FFN_SUB = V7X_MXU_COLS
ROW_SUB = 256

V7X_VMEM_LIMIT = 62 * 1024 * 1024

TM_INPROJ = 512
TM_MIXER = 512
TM_FFN = 1024
TF_FFN = 512
BB_SAMPLE = 32

BF16 = jnp.bfloat16
F32 = jnp.float32


def _layer_norm(x, g, b):
    mu = jnp.mean(x, axis=-1, keepdims=True)
    xc = x - mu
    var = jnp.mean(xc * xc, axis=-1, keepdims=True)
    return xc * lax.rsqrt(var + LN_EPS) * g + b


def _params(sem):
    return pltpu.CompilerParams(dimension_semantics=sem, vmem_limit_bytes=V7X_VMEM_LIMIT)


def _cast_specs(cast, steps):
    specs = []
    for a in cast:
        span = 1
        while a.shape[0] % (steps // span) or (a.shape[0] // (steps // span)) % 16:
            span *= 2
        specs.append(pl.BlockSpec((a.shape[0] // (steps // span), a.shape[1]),
                                  functools.partial(lambda i, sp: (i // sp, 0), sp=span)))
    return specs


def _cast_slabs(cast_in, cast_out):
    for src_ref, dst_ref in zip(cast_in, cast_out):
        dst_ref[...] = src_ref[...].astype(BF16)


def _swap_head_order(cols):
    low = lax.broadcasted_iota(jnp.int32, (1, 128), 1) < HEAD_DIM
    out = []
    for j in range(2 * N_KV):
        b, a0 = j // 2, 2 * (j % 2)
        x = cols[(a0 * 4 + b) // 2]
        y = cols[((a0 + 1) * 4 + b) // 2]
        if b % 2 == 0:
            out.append(jnp.where(low, x, pltpu.roll(y, HEAD_DIM, 1)))
        else:
            out.append(jnp.where(low, pltpu.roll(x, HEAD_DIM, 1), y))
    return out


def _inproj_kernel(x_ref, g_ref, b_ref, w_ref, bias_ref, c_ref, s1_ref, s2_ref, *rest, n_cast):
    cast_in, (h_ref, q_ref, k_ref, v_ref, u_ref) = rest[:n_cast], rest[n_cast:n_cast + 5]
    cast_out, tails = rest[n_cast + 5:2 * n_cast + 5], rest[2 * n_cast + 5:]
    _cast_slabs(cast_in, cast_out)
    tm = x_ref.shape[0]
    sub = min(tm, ROW_SUB)
    for r in range(tm // sub):
        rows = slice(r * sub, (r + 1) * sub)
        h = _layer_norm(x_ref[rows, :], g_ref[...], b_ref[...])
        h_ref[rows, :] = h
        z = jnp.dot(h.astype(BF16), w_ref[...], preferred_element_type=F32) + bias_ref[...]
        cos = c_ref[rows, :]
        s_lo = s1_ref[rows, :]
        s_hi = s2_ref[rows, :]

        def rope(zc):
            return (zc * cos + pltpu.roll(zc, 128 - ROT_DIM // 2, 1) * s_lo
                    + pltpu.roll(zc, ROT_DIM // 2, 1) * s_hi)

        q_cols = _swap_head_order([rope(z[:, c * 128:(c + 1) * 128]) for c in range(Q_COLS // 128)])
        for c, qc in enumerate(q_cols):
            q_ref[rows, c * 128:(c + 1) * 128] = (qc * QK_SCALE).astype(BF16)
        k = jnp.concatenate([rope(z[:, Q_COLS + c * 128:Q_COLS + (c + 1) * 128]) for c in range(KV_COLS // 128)], 1)
        v = z[:, Q_COLS + KV_COLS:Q_COLS + 2 * KV_COLS]
        u = z[:, Q_COLS + 2 * KV_COLS:]
        k_ref[rows, :] = k.astype(k_ref.dtype)
        v_ref[rows, :] = v.astype(v_ref.dtype)
        u_ref[rows, :] = u
        if tails and r == tm // sub - 1:
            k_tail, v_tail, u_tail = tails
            k_tail[...] = k[sub - WINDOW:]
            v_tail[...] = v[sub - WINDOW:]
            u_tail[...] = u[sub - N_META:]


def _inproj(x, ln_g, ln_b, w_in, b_in, tabs, tm, cast=(), seq_steps=None):
    rows = x.shape[0]
    steps = rows // tm
    kv_dtype = F32 if seq_steps is None else BF16
    tail_specs, tail_shapes = [], []
    if seq_steps is not None:
        n_seq = steps // seq_steps
        seq_of = lambda i: (i // seq_steps, 0, 0)
        for n_rows, width in ((WINDOW, KV_COLS), (WINDOW, KV_COLS), (N_META, POOL_W)):
            tail_specs.append(pl.BlockSpec((None, n_rows, width), seq_of))
            tail_shapes.append(jax.ShapeDtypeStruct((n_seq, n_rows, width), F32))
    t_rows = tabs[0].shape[0]
    n_t = t_rows // tm
    row = lambda i: (i, 0)
    fixed = lambda i: (0, 0)
    tab = lambda i: (i % n_t, 0)
    cast_specs = _cast_specs(cast, steps)
    return pl.pallas_call(
        functools.partial(_inproj_kernel, n_cast=len(cast)),
        grid=(steps,),
        in_specs=[
            pl.BlockSpec((tm, D_MODEL), row),
            pl.BlockSpec((1, D_MODEL), fixed),
            pl.BlockSpec((1, D_MODEL), fixed),
            pl.BlockSpec((D_MODEL, IN_COLS), fixed),
            pl.BlockSpec((1, IN_COLS), fixed),
            pl.BlockSpec((tm, 128), tab),
            pl.BlockSpec((tm, 128), tab),
            pl.BlockSpec((tm, 128), tab),
        ] + cast_specs,
        out_specs=[
            pl.BlockSpec((tm, D_MODEL), row),
            pl.BlockSpec((tm, Q_COLS), row),
            pl.BlockSpec((tm, KV_COLS), row),
            pl.BlockSpec((tm, KV_COLS), row),
            pl.BlockSpec((tm, POOL_W), row),
        ] + cast_specs + tail_specs,
        out_shape=[
            jax.ShapeDtypeStruct((rows, D_MODEL), F32),
            jax.ShapeDtypeStruct((rows, Q_COLS), BF16),
            jax.ShapeDtypeStruct((rows, KV_COLS), kv_dtype),
            jax.ShapeDtypeStruct((rows, KV_COLS), kv_dtype),
            jax.ShapeDtypeStruct((rows, POOL_W), F32),
        ] + [jax.ShapeDtypeStruct(a.shape, BF16) for a in cast] + tail_shapes,
        compiler_params=_params(("arbitrary",)),
        name="inproj",
    )(x, ln_g, ln_b, w_in, b_in, *tabs, *cast)


def _outproj_ln(mix_bf16, h, wo_ref, g_ref, b_ref):
    mix = jnp.dot(mix_bf16, wo_ref[...], preferred_element_type=F32)
    return _layer_norm(ALPHA * h + mix, g_ref[...], b_ref[...])


def _pool_group_matmul(d, g, wpool_ref, pscale_ref):
    cs = slice(g * POOL_GW, (g + 1) * POOL_GW)
    y = jnp.dot(d.astype(BF16), wpool_ref[g], preferred_element_type=F32)
    return (y * pscale_ref[:, cs]).astype(BF16)


def _prompt_mixer_kernel(sinks_ref, q_ref, k_ref, khalo_ref, kmeta_ref, v_ref, vhalo_ref, vmeta_ref,
                         u_ref, uhalo_ref, umeta_ref, wpool_ref, pscale_ref, hprev_ref, wo_ref, g_ref, b_ref,
                         h1_ref, mix_ref, wos_ref, *, tm, n_j, n_tiles):
    t = pl.program_id(0)
    j = jnp.minimum(t, n_tiles - 1) % n_j
    slot = t % 2
    n_blk = tm // BLOCK

    @pl.when(t == 0)
    def _():
        wos_ref[...] = wo_ref[...]
        mix_ref[...] = jnp.zeros(mix_ref.shape, BF16)

    n_col = D_MODEL // V7X_MXU_COLS
    pieces = [(rb, nc) for rb in range(tm // ROW_SUB) for nc in range(n_col)]
    assert len(pieces) == n_blk * N_KV
    parts = []

    def outproj_piece(idx):
        rb, nc = pieces[idx]
        rows = slice(rb * ROW_SUB, (rb + 1) * ROW_SUB)
        cols = slice(nc * V7X_MXU_COLS, (nc + 1) * V7X_MXU_COLS)
        parts.append(jnp.dot(mix_ref[1 - slot, rows, :], wos_ref[:, cols], preferred_element_type=F32))
        if nc == n_col - 1:
            proj = jnp.concatenate(parts, axis=1)
            parts.clear()
            h1_ref[rows, :] = _layer_norm(ALPHA * hprev_ref[rows, :] + proj, g_ref[...], b_ref[...])

    lane_head = lax.broadcasted_iota(jnp.int32, (1, KV_COLS), 1) // HEAD_DIM
    pad = BLOCK - N_META

    first = j == 0
    zeros = jnp.zeros((pad, KV_COLS), BF16)
    k_prev = jnp.where(first, jnp.concatenate([zeros, kmeta_ref[...].astype(BF16)], axis=0), khalo_ref[...])
    v_prev = jnp.where(first, jnp.concatenate([zeros, vmeta_ref[...].astype(BF16)], axis=0), vhalo_ref[...])
    u_prev = jnp.where(first, umeta_ref[...], uhalo_ref[...])

    n_rows = GQA * BLOCK
    qi = lax.broadcasted_iota(jnp.int32, (n_rows, 2 * BLOCK), 0) % BLOCK
    ki = lax.broadcasted_iota(jnp.int32, (n_rows, 2 * BLOCK), 1)
    band = (ki >= qi) & (ki <= qi + WINDOW)
    row_group = lax.broadcasted_iota(jnp.int32, (n_rows, 1), 0) // BLOCK
    sink_cols = []
    for hh in range(N_KV):
        sink = jnp.zeros((n_rows, 1), F32)
        for gg in range(GQA):
            sink = jnp.where(row_group == gg, sinks_ref[hh * GQA + gg] * LOG2E, sink)
        sink_cols.append(sink)

    for blk in range(n_blk):
        r0 = blk * BLOCK
        rows = slice(r0, r0 + BLOCK)

        if blk == 0:
            e = jnp.concatenate([u_prev, u_ref[0:BLOCK, :]], axis=0)
            k_band = jnp.concatenate([k_prev, k_ref[0:BLOCK, :]], axis=0)
            v_band = jnp.concatenate([v_prev, v_ref[0:BLOCK, :]], axis=0)
        else:
            e = u_ref[r0 - N_META:r0 + BLOCK, :]
            k_band = k_ref[r0 - BLOCK:r0 + BLOCK, :]
            v_band = v_ref[r0 - BLOCK:r0 + BLOCK, :]

        sums = []
        cur = e
        for step in (1, 2, 4, 8):
            cur = cur + pltpu.roll(cur, step, 0)
            sums.append(cur[:, :POOL_GW])
            cur = cur[:, POOL_GW:]
        pooled = []
        for g, w in enumerate(POOL_WINDOWS):
            cs = slice(g * POOL_GW, (g + 1) * POOL_GW)
            d = sums[g][N_META:] * (1.0 / w) - e[N_META:, cs]
            pooled.append(_pool_group_matmul(d, g, wpool_ref, pscale_ref))

        if blk == 0:
            mask = band & (ki >= jnp.where(j == 0, pad, 0))
        else:
            mask = band
        qs = jnp.concatenate([q_ref[rows, gg * KV_COLS:(gg + 1) * KV_COLS] for gg in range(GQA)], axis=0)
        pns, vbs = [], []
        for hh in range(N_KV):
            kb = jnp.where(lane_head == hh, k_band, 0)
            vbs.append(jnp.where(lane_head == hh, v_band, 0))
            s = lax.dot_general(qs, kb, (((1,), (1,)), ((), ())), preferred_element_type=F32)
            s = jnp.where(mask, s, NEG_INF)
            sink = sink_cols[hh]
            m = jnp.max(s, axis=-1, keepdims=True)
            p = jnp.exp2(s - m)
            denom = jnp.sum(p, axis=-1, keepdims=True) + jnp.exp2(sink - m)
            pns.append((p * (1.0 / denom)).astype(BF16))
            outproj_piece(blk * N_KV + hh)
        o = jnp.dot(jnp.concatenate(pns, axis=1), jnp.concatenate(vbs, axis=0), preferred_element_type=F32)
        attn = _swap_head_order([o[gg * BLOCK:(gg + 1) * BLOCK, c * 128:(c + 1) * 128]
                                 for gg in range(GQA) for c in range(KV_COLS // 128)])
        mix_ref[slot, rows, :] = jnp.concatenate([a.astype(BF16) for a in attn] + pooled, axis=1)


def _prompt_mixer(sinks, q, k, v, u, h, kmeta, vmeta, umeta, wpool, pscale, wo, ln_g, ln_b, tm):
    bsz, seq = q.shape[0], q.shape[1]
    n_j = seq // tm
    n_tiles = bsz * n_j
    cur = lambda t: jnp.minimum(t, n_tiles - 1)
    tile = lambda t: (cur(t) // n_j, cur(t) % n_j, 0)
    fixed2 = lambda t: (0, 0)
    fixed3 = lambda t: (0, 0, 0)
    kv_halo = lambda t: (cur(t) // n_j, jnp.maximum((cur(t) % n_j) * (tm // BLOCK) - 1, 0), 0)
    u_halo = lambda t: (cur(t) // n_j, jnp.maximum((cur(t) % n_j) * (tm // N_META) - 1, 0), 0)
    prev = lambda t: (jnp.maximum(t - 1, 0), 0)
    return pl.pallas_call(
        functools.partial(_prompt_mixer_kernel, tm=tm, n_j=n_j, n_tiles=n_tiles),
        grid=(n_tiles + 1,),
        in_specs=[
            pl.BlockSpec(memory_space=pltpu.SMEM),
            pl.BlockSpec((None, tm, Q_COLS), tile),
            pl.BlockSpec((None, tm, KV_COLS), tile),
            pl.BlockSpec((None, BLOCK, KV_COLS), kv_halo),
            pl.BlockSpec((N_META, KV_COLS), fixed2),
            pl.BlockSpec((None, tm, KV_COLS), tile),
            pl.BlockSpec((None, BLOCK, KV_COLS), kv_halo),
            pl.BlockSpec((N_META, KV_COLS), fixed2),
            pl.BlockSpec((None, tm, POOL_W), tile),
            pl.BlockSpec((None, N_META, POOL_W), u_halo),
            pl.BlockSpec((N_META, POOL_W), fixed2),
            pl.BlockSpec((len(POOL_WINDOWS), POOL_GW, POOL_GW), fixed3),
            pl.BlockSpec((1, POOL_W), fixed2),
            pl.BlockSpec((tm, D_MODEL), prev),
            pl.BlockSpec((D_MODEL, D_MODEL), fixed2, pipeline_mode=pl.Buffered(1)),
            pl.BlockSpec((1, D_MODEL), fixed2),
            pl.BlockSpec((1, D_MODEL), fixed2),
        ],
        out_specs=pl.BlockSpec((tm, D_MODEL), prev),
        out_shape=jax.ShapeDtypeStruct((bsz * seq, D_MODEL), F32),
        scratch_shapes=[
            pltpu.VMEM((2, tm, D_MODEL), BF16),
            pltpu.VMEM((D_MODEL, D_MODEL), BF16),
        ],
        compiler_params=_params(("arbitrary",)),
        name="prompt_mixer",
    )(sinks, q, k, k, kmeta, v, v, vmeta, u, u, umeta, wpool, pscale, h, wo, ln_g, ln_b)


def _sample_attn_kernel(sinks_ref, q_ref, kn_ref, vn_ref, ck_ref, cv_ref,
                        attn_ref, nk_ref, nv_ref):
    bb = q_ref.shape[0]
    wk = ck_ref.shape[1]
    ck = ck_ref[...]
    cv = cv_ref[...]
    kn = kn_ref[...]
    vn = vn_ref[...]
    head_of_lane = lax.broadcasted_iota(jnp.int32, (KV_COLS, 128), 0) // HEAD_DIM
    out_lane = lax.broadcasted_iota(jnp.int32, (KV_COLS, 128), 1)
    seg = (head_of_lane == out_lane).astype(BF16)
    lane_t = lax.broadcasted_iota(jnp.int32, (128, KV_COLS), 0)
    head_t = lax.broadcasted_iota(jnp.int32, (128, KV_COLS), 1) // HEAD_DIM
    spread = (lane_t == head_t).astype(BF16)
    lane = lax.broadcasted_iota(jnp.int32, (1, 128), 1)

    o_cols = []
    for gg in range(GQA):
        qg = q_ref[:, gg * KV_COLS:(gg + 1) * KV_COLS].astype(F32)
        prod = (ck * qg[:, None, :]).astype(BF16).reshape(bb * wk, KV_COLS)
        s = jnp.dot(prod, seg, preferred_element_type=F32).reshape(bb, wk, 128)
        s_new = jnp.dot((kn * qg).astype(BF16), seg, preferred_element_type=F32)
        sink = jnp.zeros((1, 128), F32)
        for hh in range(N_KV):
            sink = jnp.where(lane == hh, sinks_ref[hh * GQA + gg] * LOG2E, sink)
        m = jnp.maximum(jnp.maximum(jnp.max(s, axis=1), s_new), sink)
        p = jnp.exp2(s - m[:, None, :])
        p_new = jnp.exp2(s_new - m)
        denom = jnp.sum(p, axis=1) + p_new + jnp.exp2(sink - m)
        r = 1.0 / denom
        pn = (p * r[:, None, :]).astype(BF16).reshape(bb * wk, 128)
        pn_new = (p_new * r).astype(BF16)
        pe = jnp.dot(pn, spread, preferred_element_type=F32).reshape(bb, wk, KV_COLS)
        pe_new = jnp.dot(pn_new, spread, preferred_element_type=F32)
        o = jnp.sum(pe * cv, axis=1) + pe_new * vn
        o_cols += [o[:, c * 128:(c + 1) * 128] for c in range(KV_COLS // 128)]
    for c, oc in enumerate(_swap_head_order(o_cols)):
        attn_ref[:, c * 128:(c + 1) * 128] = oc.astype(BF16)

    nk_ref[:, 0:wk - 1, :] = ck_ref[:, 1:wk, :]
    nk_ref[:, wk - 1:wk, :] = kn[:, None, :]
    nv_ref[:, 0:wk - 1, :] = cv_ref[:, 1:wk, :]
    nv_ref[:, wk - 1:wk, :] = vn[:, None, :]


def _sample_attn(sinks, q, kn, vn, ck, cv, bb):
    bsz, wk = ck.shape[0], ck.shape[1]
    row = lambda i: (i, 0)
    cache = lambda i: (i, 0, 0)
    return pl.pallas_call(
        _sample_attn_kernel,
        grid=(bsz // bb,),
        in_specs=[
            pl.BlockSpec(memory_space=pltpu.SMEM),
            pl.BlockSpec((bb, Q_COLS), row),
            pl.BlockSpec((bb, KV_COLS), row),
            pl.BlockSpec((bb, KV_COLS), row),
            pl.BlockSpec((bb, wk, KV_COLS), cache),
            pl.BlockSpec((bb, wk, KV_COLS), cache),
        ],
        out_specs=[
            pl.BlockSpec((bb, Q_COLS), row),
            pl.BlockSpec((bb, wk, KV_COLS), cache),
            pl.BlockSpec((bb, wk, KV_COLS), cache),
        ],
        out_shape=[
            jax.ShapeDtypeStruct((bsz, Q_COLS), BF16),
            jax.ShapeDtypeStruct((bsz, wk, KV_COLS), F32),
            jax.ShapeDtypeStruct((bsz, wk, KV_COLS), F32),
        ],
        compiler_params=_params(("parallel",)),
        name="sample_attn",
    )(sinks, q, kn, vn, ck, cv)


def _sample_mix_kernel(attn_ref, u_ref, st_ref, h_ref, wpool_ref, pscale_ref, wo_ref, g_ref, b_ref,
                       h1_ref, nst_ref, mix_ref):
    u = u_ref[...]
    mix_ref[:, 0:Q_COLS] = attn_ref[...]
    for g, w in enumerate(POOL_WINDOWS):
        cs = slice(g * POOL_GW, (g + 1) * POOL_GW)
        acc = u[:, cs]
        for t in range(POOL_HIST - (w - 1), POOL_HIST):
            acc = acc + st_ref[t, :, cs]
        d = acc * (1.0 / w) - u[:, cs]
        mix_ref[:, Q_COLS + g * POOL_GW:Q_COLS + (g + 1) * POOL_GW] = _pool_group_matmul(d, g, wpool_ref, pscale_ref)
    nst_ref[0:POOL_HIST - 1] = st_ref[1:POOL_HIST]
    nst_ref[POOL_HIST - 1] = u
    h1_ref[...] = _outproj_ln(mix_ref[...], h_ref[...], wo_ref, g_ref, b_ref)


def _sample_mix(attn, u, state, h, wpool, pscale, wo, ln_g, ln_b):
    bsz = attn.shape[0]
    return pl.pallas_call(
        _sample_mix_kernel,
        out_shape=[
            jax.ShapeDtypeStruct((bsz, D_MODEL), F32),
            jax.ShapeDtypeStruct((POOL_HIST, bsz, POOL_W), F32),
        ],
        scratch_shapes=[pltpu.VMEM((bsz, D_MODEL), BF16)],
        compiler_params=pltpu.CompilerParams(vmem_limit_bytes=V7X_VMEM_LIMIT),
        name="sample_mix",
    )(attn, u, state, h, wpool, pscale, wo, ln_g, ln_b)


def _ffn_kernel(h_ref, wg_ref, wu_ref, wd_ref, g_ref, b_ref, o_ref):
    f = pl.program_id(1)
    last = pl.num_programs(1) - 1
    tm = h_ref.shape[0]
    tf = wg_ref.shape[1]

    def activation(hb, c):
        cs = slice(c * FFN_SUB, (c + 1) * FFN_SUB)
        gate = jnp.dot(hb, wg_ref[:, cs], preferred_element_type=F32)
        up = jnp.dot(hb, wu_ref[:, cs], preferred_element_type=F32)
        return (gate * (1.0 / (1.0 + jnp.exp(-gate))) * up).astype(BF16)

    def chunk_sum(hb):
        part = None
        for c in range(tf // FFN_SUB):
            p = jnp.dot(activation(hb, c), wd_ref[c * FFN_SUB:(c + 1) * FFN_SUB, :], preferred_element_type=F32)
            part = p if part is None else part + p
        return part

    @pl.when(f == 0)
    def _():
        h = h_ref[...]
        o_ref[...] = ALPHA * h + chunk_sum(h.astype(BF16))

    @pl.when((f > 0) & (f < last))
    def _():
        o_ref[...] += chunk_sum(h_ref[...].astype(BF16))

    @pl.when(f == last)
    def _():
        hb = h_ref[...].astype(BF16)
        a = jnp.concatenate([activation(hb, c) for c in range(tf // FFN_SUB)], axis=1)
        sub = min(tm, ROW_SUB)
        for r in range(tm // sub):
            rows = slice(r * sub, (r + 1) * sub)
            p = jnp.dot(a[rows], wd_ref[...], preferred_element_type=F32)
            o_ref[rows, :] = _layer_norm(o_ref[rows, :] + p, g_ref[...], b_ref[...])


def _ffn(h1, wg, wu, wd, ln_g, ln_b, tm, tf):
    rows = h1.shape[0]
    return pl.pallas_call(
        _ffn_kernel,
        grid=(rows // tm, D_FF // tf),
        in_specs=[
            pl.BlockSpec((tm, D_MODEL), lambda i, f: (i, 0)),
            pl.BlockSpec((D_MODEL, tf), lambda i, f: (0, f)),
            pl.BlockSpec((D_MODEL, tf), lambda i, f: (0, f)),
            pl.BlockSpec((tf, D_MODEL), lambda i, f: (f, 0)),
            pl.BlockSpec((1, D_MODEL), lambda i, f: (0, 0)),
            pl.BlockSpec((1, D_MODEL), lambda i, f: (0, 0)),
        ],
        out_specs=pl.BlockSpec((tm, D_MODEL), lambda i, f: (i, 0)),
        out_shape=jax.ShapeDtypeStruct((rows, D_MODEL), F32),
        compiler_params=_params(("parallel", "arbitrary")),
        name="ffn",
    )(h1, wg, wu, wd, ln_g, ln_b)


def _rope_tables(pos):
    half = ROT_DIM // 2
    inv_freq = ROPE_THETA ** (-np.arange(half, dtype=np.float64) * 2.0 / ROT_DIM)
    ang = np.asarray(pos, np.float64)[:, None] * inv_freq
    cos, sin = np.cos(ang), np.sin(ang)
    t = ang.shape[0]
    ones = np.ones((t, HEAD_DIM - ROT_DIM))
    zeros = np.zeros((t, HEAD_DIM - ROT_DIM))
    zh = np.zeros((t, half))
    c = np.concatenate([cos, cos, ones], -1)
    s_lo = np.concatenate([-sin, zh, zeros], -1)
    s_hi = np.concatenate([zh, sin, zeros], -1)
    return tuple(jnp.asarray(np.concatenate([a, a], -1), F32) for a in (c, s_lo, s_hi))


def kernel(x_prompt, x_sample, cache_k, cache_v, state_pool, meta_tokens, ln_in_g, ln_in_b, w_in, b_in,
           attn_sinks, w_pool, pool_scale, w_o, ln1_g, ln1_b, w_gate, w_up, w_down, ln2_g, ln2_b):
    bp, seq, _ = x_prompt.shape
    bs = x_sample.shape[0]
    wk = cache_k.shape[2]
    assert x_sample.shape[1] == 1 and cache_k.shape[0] == 1

    w_in_p = w_in[0].astype(BF16)
    b_in_p = b_in[0][None]
    wpool = w_pool[0].astype(BF16)
    pscale = pool_scale[0][None]
    sinks = attn_sinks[0]
    g_in, b_ln_in = ln_in_g[None], ln_in_b[None]
    g1, b1, g2, b2 = ln1_g[0][None], ln1_b[0][None], ln2_g[0][None], ln2_b[0][None]

    tabs_prompt = _rope_tables(N_META + np.arange(seq))
    tabs_small = _rope_tables(np.concatenate([np.arange(N_META), np.full((bs,), PAST_LEN)]))

    x_small = jnp.concatenate([meta_tokens, x_sample.reshape(bs, D_MODEL)], 0)
    h_ms, q_ms, k_ms, v_ms, u_ms = _inproj(x_small, g_in, b_ln_in, w_in_p, b_in_p, tabs_small, N_META + bs)
    k_meta, v_meta, u_meta = k_ms[:N_META], v_ms[:N_META], u_ms[:N_META]
    h_s, q_s, k_s, v_s, u_s = h_ms[N_META:], q_ms[N_META:], k_ms[N_META:], v_ms[N_META:], u_ms[N_META:]

    h_p, q_p, k_p, v_p, u_p, wg, wu, wd, w_o_p, k_tail, v_tail, u_tail = _inproj(
        x_prompt.reshape(bp * seq, D_MODEL), g_in, b_ln_in, w_in_p, b_in_p, tabs_prompt, TM_INPROJ,
        cast=(w_gate[0], w_up[0], w_down[0], w_o[0]), seq_steps=seq // TM_INPROJ)
    k_p3 = k_p.reshape(bp, seq, KV_COLS)
    v_p3 = v_p.reshape(bp, seq, KV_COLS)
    u_p3 = u_p.reshape(bp, seq, POOL_W)
    h1_p = _prompt_mixer(sinks, q_p.reshape(bp, seq, Q_COLS), k_p3, v_p3, u_p3, h_p,
                         k_meta, v_meta, u_meta, wpool, pscale, w_o_p, g1, b1, TM_MIXER)
    y_prompt = _ffn(h1_p, wg, wu, wd, g2, b2, TM_FFN, TF_FFN).reshape(bp, seq, D_MODEL)

    attn_s, nk_s, nv_s = _sample_attn(sinks, q_s, k_s, v_s, cache_k.reshape(bs, wk, KV_COLS),
                                      cache_v.reshape(bs, wk, KV_COLS), BB_SAMPLE)
    h1_s, npool_s = _sample_mix(attn_s, u_s, state_pool[0].swapaxes(0, 1), h_s, wpool, pscale, w_o_p, g1, b1)
    y_sample = _ffn(h1_s, wg, wu, wd, g2, b2, bs, TF_FFN).reshape(bs, 1, D_MODEL)

    new_k_prompt = k_tail.reshape(1, bp, WINDOW, N_KV, HEAD_DIM)
    new_v_prompt = v_tail.reshape(1, bp, WINDOW, N_KV, HEAD_DIM)
    new_pool_prompt = u_tail[:, N_META - POOL_HIST:][None]
    return (y_prompt, y_sample, new_k_prompt, new_v_prompt, new_pool_prompt,
            nk_s.reshape(1, bs, wk, N_KV, HEAD_DIM), nv_s.reshape(1, bs, wk, N_KV, HEAD_DIM), npool_s.swapaxes(0, 1)[None])
```

```python
import functools
import math

import jax
import numpy as np
import jax.numpy as jnp
from jax import lax
from jax.experimental import pallas as pl
from jax.experimental.pallas import tpu as pltpu

D_MODEL = 2048
N_META = 16
HEAD_DIM = 64
N_KV = 4
GQA = 4
Q_COLS = N_KV * GQA * HEAD_DIM
KV_COLS = N_KV * HEAD_DIM
POOL_W = 1024
POOL_WINDOWS = (2, 4, 8, 16)
POOL_GW = POOL_W // len(POOL_WINDOWS)
POOL_HIST = 15
IN_COLS = Q_COLS + 2 * KV_COLS + POOL_W
D_FF = 5632
WINDOW = 128
BLOCK = 128
ROT_DIM = 16
ROPE_THETA = 500000.0
PAST_LEN = 16384
ALPHA = 2.0 ** 0.25
LN_EPS = 1e-5
NEG_INF = -1e30
LOG2E = math.log2(math.e)
QK_SCALE = LOG2E / math.sqrt(HEAD_DIM)
V7X_MXU_COLS = 256
FFN_SUB = V7X_MXU_COLS
ROW_SUB = 256

V7X_VMEM_LIMIT = 62 * 1024 * 1024

TM_INPROJ = 512
TM_MIXER = 512
TM_FFN = 1024
TF_FFN = 512
BB_SAMPLE = 32

BF16 = jnp.bfloat16
F32 = jnp.float32


def _layer_norm(x, g, b):
    mu = jnp.mean(x, axis=-1, keepdims=True)
    xc = x - mu
    var = jnp.mean(xc * xc, axis=-1, keepdims=True)
    return xc * lax.rsqrt(var + LN_EPS) * g + b


def _params(sem):
    return pltpu.CompilerParams(dimension_semantics=sem, vmem_limit_bytes=V7X_VMEM_LIMIT)


def _cast_specs(cast, steps):
    specs = []
    for a in cast:
        span = 1
        while a.shape[0] % (steps // span) or (a.shape[0] // (steps // span)) % 16:
            span *= 2
        specs.append(pl.BlockSpec((a.shape[0] // (steps // span), a.shape[1]),
                                  functools.partial(lambda i, sp: (i // sp, 0), sp=span)))
    return specs


def _cast_slabs(cast_in, cast_out):
    for src_ref, dst_ref in zip(cast_in, cast_out):
        dst_ref[...] = src_ref[...].astype(BF16)


def _swap_head_order(cols):
    low = lax.broadcasted_iota(jnp.int32, (1, 128), 1) < HEAD_DIM
    out = []
    for j in range(2 * N_KV):
        b, a0 = j // 2, 2 * (j % 2)
        x = cols[(a0 * 4 + b) // 2]
        y = cols[((a0 + 1) * 4 + b) // 2]
        if b % 2 == 0:
            out.append(jnp.where(low, x, pltpu.roll(y, HEAD_DIM, 1)))
        else:
            out.append(jnp.where(low, pltpu.roll(x, HEAD_DIM, 1), y))
    return out


def _inproj_kernel(x_ref, g_ref, b_ref, w_ref, bias_ref, c_ref, s1_ref, s2_ref, *rest, n_cast):
    cast_in, (h_ref, q_ref, k_ref, v_ref, u_ref) = rest[:n_cast], rest[n_cast:n_cast + 5]
    cast_out, tails = rest[n_cast + 5:2 * n_cast + 5], rest[2 * n_cast + 5:]
    _cast_slabs(cast_in, cast_out)
    tm = x_ref.shape[0]
    sub = min(tm, ROW_SUB)
    for r in range(tm // sub):
        rows = slice(r * sub, (r + 1) * sub)
        h = _layer_norm(x_ref[rows, :], g_ref[...], b_ref[...])
        h_ref[rows, :] = h
        z = jnp.dot(h.astype(BF16), w_ref[...], preferred_element_type=F32) + bias_ref[...]
        cos = c_ref[rows, :]
        s_lo = s1_ref[rows, :]
        s_hi = s2_ref[rows, :]

        def rope(zc):
            return (zc * cos + pltpu.roll(zc, 128 - ROT_DIM // 2, 1) * s_lo
                    + pltpu.roll(zc, ROT_DIM // 2, 1) * s_hi)

        q_cols = _swap_head_order([rope(z[:, c * 128:(c + 1) * 128]) for c in range(Q_COLS // 128)])
        for c, qc in enumerate(q_cols):
            q_ref[rows, c * 128:(c + 1) * 128] = (qc * QK_SCALE).astype(BF16)
        k = jnp.concatenate([rope(z[:, Q_COLS + c * 128:Q_COLS + (c + 1) * 128]) for c in range(KV_COLS // 128)], 1)
        v = z[:, Q_COLS + KV_COLS:Q_COLS + 2 * KV_COLS]
        u = z[:, Q_COLS + 2 * KV_COLS:]
        k_ref[rows, :] = k.astype(k_ref.dtype)
        v_ref[rows, :] = v.astype(v_ref.dtype)
        u_ref[rows, :] = u
        if tails and r == tm // sub - 1:
            k_tail, v_tail, u_tail = tails
            k_tail[...] = k[sub - WINDOW:]
            v_tail[...] = v[sub - WINDOW:]
            u_tail[...] = u[sub - N_META:]


def _inproj(x, ln_g, ln_b, w_in, b_in, tabs, tm, cast=(), seq_steps=None):
    rows = x.shape[0]
    steps = rows // tm
    kv_dtype = F32 if seq_steps is None else BF16
    tail_specs, tail_shapes = [], []
    if seq_steps is not None:
        n_seq = steps // seq_steps
        seq_of = lambda i: (i // seq_steps, 0, 0)
        for n_rows, width in ((WINDOW, KV_COLS), (WINDOW, KV_COLS), (N_META, POOL_W)):
            tail_specs.append(pl.BlockSpec((None, n_rows, width), seq_of))
            tail_shapes.append(jax.ShapeDtypeStruct((n_seq, n_rows, width), F32))
    t_rows = tabs[0].shape[0]
    n_t = t_rows // tm
    row = lambda i: (i, 0)
    fixed = lambda i: (0, 0)
    tab = lambda i: (i % n_t, 0)
    cast_specs = _cast_specs(cast, steps)
    return pl.pallas_call(
        functools.partial(_inproj_kernel, n_cast=len(cast)),
        grid=(steps,),
        in_specs=[
            pl.BlockSpec((tm, D_MODEL), row),
            pl.BlockSpec((1, D_MODEL), fixed),
            pl.BlockSpec((1, D_MODEL), fixed),
            pl.BlockSpec((D_MODEL, IN_COLS), fixed),
            pl.BlockSpec((1, IN_COLS), fixed),
            pl.BlockSpec((tm, 128), tab),
            pl.BlockSpec((tm, 128), tab),
            pl.BlockSpec((tm, 128), tab),
        ] + cast_specs,
        out_specs=[
            pl.BlockSpec((tm, D_MODEL), row),
            pl.BlockSpec((tm, Q_COLS), row),
            pl.BlockSpec((tm, KV_COLS), row),
            pl.BlockSpec((tm, KV_COLS), row),
            pl.BlockSpec((tm, POOL_W), row),
        ] + cast_specs + tail_specs,
        out_shape=[
            jax.ShapeDtypeStruct((rows, D_MODEL), F32),
            jax.ShapeDtypeStruct((rows, Q_COLS), BF16),
            jax.ShapeDtypeStruct((rows, KV_COLS), kv_dtype),
            jax.ShapeDtypeStruct((rows, KV_COLS), kv_dtype),
            jax.ShapeDtypeStruct((rows, POOL_W), F32),
        ] + [jax.ShapeDtypeStruct(a.shape, BF16) for a in cast] + tail_shapes,
        compiler_params=_params(("arbitrary",)),
        name="inproj",
    )(x, ln_g, ln_b, w_in, b_in, *tabs, *cast)


def _outproj_ln(mix_bf16, h, wo_ref, g_ref, b_ref):
    mix = jnp.dot(mix_bf16, wo_ref[...], preferred_element_type=F32)
    return _layer_norm(ALPHA * h + mix, g_ref[...], b_ref[...])


def _pool_group_matmul(d, g, wpool_ref, pscale_ref):
    cs = slice(g * POOL_GW, (g + 1) * POOL_GW)
    y = jnp.dot(d.astype(BF16), wpool_ref[g], preferred_element_type=F32)
    return (y * pscale_ref[:, cs]).astype(BF16)


def _prompt_mixer_kernel(sinks_ref, q_ref, k_ref, khalo_ref, kmeta_ref, v_ref, vhalo_ref, vmeta_ref,
                         u_ref, uhalo_ref, umeta_ref, wpool_ref, pscale_ref, hprev_ref, wo_ref, g_ref, b_ref,
                         h1_ref, mix_ref, wos_ref, *, tm, n_j, n_tiles):
    t = pl.program_id(0)
    j = jnp.minimum(t, n_tiles - 1) % n_j
    slot = t % 2
    n_blk = tm // BLOCK

    @pl.when(t == 0)
    def _():
        wos_ref[...] = wo_ref[...]
        mix_ref[...] = jnp.zeros(mix_ref.shape, BF16)

    n_col = D_MODEL // V7X_MXU_COLS
    pieces = [(rb, nc) for rb in range(tm // ROW_SUB) for nc in range(n_col)]
    assert len(pieces) == n_blk * N_KV
    parts = []

    def outproj_piece(idx):
        rb, nc = pieces[idx]
        rows = slice(rb * ROW_SUB, (rb + 1) * ROW_SUB)
        cols = slice(nc * V7X_MXU_COLS, (nc + 1) * V7X_MXU_COLS)
        parts.append(jnp.dot(mix_ref[1 - slot, rows, :], wos_ref[:, cols], preferred_element_type=F32))
        if nc == n_col - 1:
            proj = jnp.concatenate(parts, axis=1)
            parts.clear()
            h1_ref[rows, :] = _layer_norm(ALPHA * hprev_ref[rows, :] + proj, g_ref[...], b_ref[...])

    lane_head = lax.broadcasted_iota(jnp.int32, (1, KV_COLS), 1) // HEAD_DIM
    pad = BLOCK - N_META

    first = j == 0
    zeros = jnp.zeros((pad, KV_COLS), BF16)
    k_prev = jnp.where(first, jnp.concatenate([zeros, kmeta_ref[...].astype(BF16)], axis=0), khalo_ref[...])
    v_prev = jnp.where(first, jnp.concatenate([zeros, vmeta_ref[...].astype(BF16)], axis=0), vhalo_ref[...])
    u_prev = jnp.where(first, umeta_ref[...], uhalo_ref[...])

    n_rows = GQA * BLOCK
    qi = lax.broadcasted_iota(jnp.int32, (2 * BLOCK, n_rows), 1) % BLOCK
    ki = lax.broadcasted_iota(jnp.int32, (2 * BLOCK, n_rows), 0)
    band = (ki >= qi) & (ki <= qi + WINDOW)
    col_group = lax.broadcasted_iota(jnp.int32, (1, n_rows), 1) // BLOCK
    sink_rows = []
    for hh in range(N_KV):
        sink = jnp.zeros((1, n_rows), F32)
        for gg in range(GQA):
            sink = jnp.where(col_group == gg, sinks_ref[hh * GQA + gg] * LOG2E, sink)
        sink_rows.append(sink)

    for blk in range(n_blk):
        r0 = blk * BLOCK
        rows = slice(r0, r0 + BLOCK)

        if blk == 0:
            e = jnp.concatenate([u_prev, u_ref[0:BLOCK, :]], axis=0)
            k_band = jnp.concatenate([k_prev, k_ref[0:BLOCK, :]], axis=0)
            v_band = jnp.concatenate([v_prev, v_ref[0:BLOCK, :]], axis=0)
        else:
            e = u_ref[r0 - N_META:r0 + BLOCK, :]
            k_band = k_ref[r0 - BLOCK:r0 + BLOCK, :]
            v_band = v_ref[r0 - BLOCK:r0 + BLOCK, :]

        sums = []
        cur = e
        for step in (1, 2, 4, 8):
            cur = cur + pltpu.roll(cur, step, 0)
            sums.append(cur[:, :POOL_GW])
            cur = cur[:, POOL_GW:]
        pooled = []
        for g, w in enumerate(POOL_WINDOWS):
            cs = slice(g * POOL_GW, (g + 1) * POOL_GW)
            d = sums[g][N_META:] * (1.0 / w) - e[N_META:, cs]
            pooled.append(_pool_group_matmul(d, g, wpool_ref, pscale_ref))

        if blk == 0:
            mask = band & (ki >= jnp.where(j == 0, pad, 0))
        else:
            mask = band
        qs = jnp.concatenate([q_ref[rows, gg * KV_COLS:(gg + 1) * KV_COLS] for gg in range(GQA)], axis=0)
        pns, vbs = [], []
        for hh in range(N_KV):
            qh = jnp.where(lane_head == hh, qs, 0)
            vbs.append(jnp.where(lane_head == hh, v_band, 0))
            s = lax.dot_general(k_band, qh, (((1,), (1,)), ((), ())), preferred_element_type=F32)
            s = jnp.where(mask, s, NEG_INF)
            sink = sink_rows[hh]
            m = jnp.max(s, axis=0, keepdims=True)
            p = jnp.exp2(s - m)
            denom = jnp.sum(p, axis=0, keepdims=True) + jnp.exp2(sink - m)
            pns.append((p * (1.0 / denom)).astype(BF16))
            outproj_piece(blk * N_KV + hh)
        o = lax.dot_general(jnp.concatenate(pns, axis=0), jnp.concatenate(vbs, axis=0),
                            (((0,), (0,)), ((), ())), preferred_element_type=F32)
        attn = _swap_head_order([o[gg * BLOCK:(gg + 1) * BLOCK, c * 128:(c + 1) * 128]
                                 for gg in range(GQA) for c in range(KV_COLS // 128)])
        mix_ref[slot, rows, :] = jnp.concatenate([a.astype(BF16) for a in attn] + pooled, axis=1)


def _prompt_mixer(sinks, q, k, v, u, h, kmeta, vmeta, umeta, wpool, pscale, wo, ln_g, ln_b, tm):
    bsz, seq = q.shape[0], q.shape[1]
    n_j = seq // tm
    n_tiles = bsz * n_j
    cur = lambda t: jnp.minimum(t, n_tiles - 1)
    tile = lambda t: (cur(t) // n_j, cur(t) % n_j, 0)
    fixed2 = lambda t: (0, 0)
    fixed3 = lambda t: (0, 0, 0)
    kv_halo = lambda t: (cur(t) // n_j, jnp.maximum((cur(t) % n_j) * (tm // BLOCK) - 1, 0), 0)
    u_halo = lambda t: (cur(t) // n_j, jnp.maximum((cur(t) % n_j) * (tm // N_META) - 1, 0), 0)
    prev = lambda t: (jnp.maximum(t - 1, 0), 0)
    return pl.pallas_call(
        functools.partial(_prompt_mixer_kernel, tm=tm, n_j=n_j, n_tiles=n_tiles),
        grid=(n_tiles + 1,),
        in_specs=[
            pl.BlockSpec(memory_space=pltpu.SMEM),
            pl.BlockSpec((None, tm, Q_COLS), tile),
            pl.BlockSpec((None, tm, KV_COLS), tile),
            pl.BlockSpec((None, BLOCK, KV_COLS), kv_halo),
            pl.BlockSpec((N_META, KV_COLS), fixed2),
            pl.BlockSpec((None, tm, KV_COLS), tile),
            pl.BlockSpec((None, BLOCK, KV_COLS), kv_halo),
            pl.BlockSpec((N_META, KV_COLS), fixed2),
            pl.BlockSpec((None, tm, POOL_W), tile),
            pl.BlockSpec((None, N_META, POOL_W), u_halo),
            pl.BlockSpec((N_META, POOL_W), fixed2),
            pl.BlockSpec((len(POOL_WINDOWS), POOL_GW, POOL_GW), fixed3),
            pl.BlockSpec((1, POOL_W), fixed2),
            pl.BlockSpec((tm, D_MODEL), prev),
            pl.BlockSpec((D_MODEL, D_MODEL), fixed2, pipeline_mode=pl.Buffered(1)),
            pl.BlockSpec((1, D_MODEL), fixed2),
            pl.BlockSpec((1, D_MODEL), fixed2),
        ],
        out_specs=pl.BlockSpec((tm, D_MODEL), prev),
        out_shape=jax.ShapeDtypeStruct((bsz * seq, D_MODEL), F32),
        scratch_shapes=[
            pltpu.VMEM((2, tm, D_MODEL), BF16),
            pltpu.VMEM((D_MODEL, D_MODEL), BF16),
        ],
        compiler_params=_params(("arbitrary",)),
        name="prompt_mixer",
    )(sinks, q, k, k, kmeta, v, v, vmeta, u, u, umeta, wpool, pscale, h, wo, ln_g, ln_b)


def _sample_attn_kernel(sinks_ref, q_ref, kn_ref, vn_ref, ck_ref, cv_ref,
                        attn_ref, nk_ref, nv_ref):
    bb = q_ref.shape[0]
    wk = ck_ref.shape[1]
    ck = ck_ref[...]
    cv = cv_ref[...]
    kn = kn_ref[...]
    vn = vn_ref[...]
    head_of_lane = lax.broadcasted_iota(jnp.int32, (KV_COLS, 128), 0) // HEAD_DIM
    out_lane = lax.broadcasted_iota(jnp.int32, (KV_COLS, 128), 1)
    seg = (head_of_lane == out_lane).astype(BF16)
    lane_t = lax.broadcasted_iota(jnp.int32, (128, KV_COLS), 0)
    head_t = lax.broadcasted_iota(jnp.int32, (128, KV_COLS), 1) // HEAD_DIM
    spread = (lane_t == head_t).astype(BF16)
    lane = lax.broadcasted_iota(jnp.int32, (1, 128), 1)

    o_cols = []
    for gg in range(GQA):
        qg = q_ref[:, gg * KV_COLS:(gg + 1) * KV_COLS].astype(F32)
        prod = (ck * qg[:, None, :]).astype(BF16).reshape(bb * wk, KV_COLS)
        s = jnp.dot(prod, seg, preferred_element_type=F32).reshape(bb, wk, 128)
        s_new = jnp.dot((kn * qg).astype(BF16), seg, preferred_element_type=F32)
        sink = jnp.zeros((1, 128), F32)
        for hh in range(N_KV):
            sink = jnp.where(lane == hh, sinks_ref[hh * GQA + gg] * LOG2E, sink)
        m = jnp.maximum(jnp.maximum(jnp.max(s, axis=1), s_new), sink)
        p = jnp.exp2(s - m[:, None, :])
        p_new = jnp.exp2(s_new - m)
        denom = jnp.sum(p, axis=1) + p_new + jnp.exp2(sink - m)
        r = 1.0 / denom
        pn = (p * r[:, None, :]).astype(BF16).reshape(bb * wk, 128)
        pn_new = (p_new * r).astype(BF16)
        pe = jnp.dot(pn, spread, preferred_element_type=F32).reshape(bb, wk, KV_COLS)
        pe_new = jnp.dot(pn_new, spread, preferred_element_type=F32)
        o = jnp.sum(pe * cv, axis=1) + pe_new * vn
        o_cols += [o[:, c * 128:(c + 1) * 128] for c in range(KV_COLS // 128)]
    for c, oc in enumerate(_swap_head_order(o_cols)):
        attn_ref[:, c * 128:(c + 1) * 128] = oc.astype(BF16)

    nk_ref[:, 0:wk - 1, :] = ck_ref[:, 1:wk, :]
    nk_ref[:, wk - 1:wk, :] = kn[:, None, :]
    nv_ref[:, 0:wk - 1, :] = cv_ref[:, 1:wk, :]
    nv_ref[:, wk - 1:wk, :] = vn[:, None, :]


def _sample_attn(sinks, q, kn, vn, ck, cv, bb):
    bsz, wk = ck.shape[0], ck.shape[1]
    row = lambda i: (i, 0)
    cache = lambda i: (i, 0, 0)
    return pl.pallas_call(
        _sample_attn_kernel,
        grid=(bsz // bb,),
        in_specs=[
            pl.BlockSpec(memory_space=pltpu.SMEM),
            pl.BlockSpec((bb, Q_COLS), row),
            pl.BlockSpec((bb, KV_COLS), row),
            pl.BlockSpec((bb, KV_COLS), row),
            pl.BlockSpec((bb, wk, KV_COLS), cache),
            pl.BlockSpec((bb, wk, KV_COLS), cache),
        ],
        out_specs=[
            pl.BlockSpec((bb, Q_COLS), row),
            pl.BlockSpec((bb, wk, KV_COLS), cache),
            pl.BlockSpec((bb, wk, KV_COLS), cache),
        ],
        out_shape=[
            jax.ShapeDtypeStruct((bsz, Q_COLS), BF16),
            jax.ShapeDtypeStruct((bsz, wk, KV_COLS), F32),
            jax.ShapeDtypeStruct((bsz, wk, KV_COLS), F32),
        ],
        compiler_params=_params(("parallel",)),
        name="sample_attn",
    )(sinks, q, kn, vn, ck, cv)


def _sample_mix_kernel(attn_ref, u_ref, st_ref, h_ref, wpool_ref, pscale_ref, wo_ref, g_ref, b_ref,
                       h1_ref, nst_ref, mix_ref):
    u = u_ref[...]
    mix_ref[:, 0:Q_COLS] = attn_ref[...]
    for g, w in enumerate(POOL_WINDOWS):
        cs = slice(g * POOL_GW, (g + 1) * POOL_GW)
        acc = u[:, cs]
        for t in range(POOL_HIST - (w - 1), POOL_HIST):
            acc = acc + st_ref[t, :, cs]
        d = acc * (1.0 / w) - u[:, cs]
        mix_ref[:, Q_COLS + g * POOL_GW:Q_COLS + (g + 1) * POOL_GW] = _pool_group_matmul(d, g, wpool_ref, pscale_ref)
    nst_ref[0:POOL_HIST - 1] = st_ref[1:POOL_HIST]
    nst_ref[POOL_HIST - 1] = u
    h1_ref[...] = _outproj_ln(mix_ref[...], h_ref[...], wo_ref, g_ref, b_ref)


def _sample_mix(attn, u, state, h, wpool, pscale, wo, ln_g, ln_b):
    bsz = attn.shape[0]
    return pl.pallas_call(
        _sample_mix_kernel,
        out_shape=[
            jax.ShapeDtypeStruct((bsz, D_MODEL), F32),
            jax.ShapeDtypeStruct((POOL_HIST, bsz, POOL_W), F32),
        ],
        scratch_shapes=[pltpu.VMEM((bsz, D_MODEL), BF16)],
        compiler_params=pltpu.CompilerParams(vmem_limit_bytes=V7X_VMEM_LIMIT),
        name="sample_mix",
    )(attn, u, state, h, wpool, pscale, wo, ln_g, ln_b)


def _ffn_kernel(h_ref, wg_ref, wu_ref, wd_ref, g_ref, b_ref, o_ref):
    f = pl.program_id(1)
    last = pl.num_programs(1) - 1
    tm = h_ref.shape[0]
    tf = wg_ref.shape[1]

    @pl.when(f == 0)
    def _():
        o_ref[...] = ALPHA * h_ref[...]

    def activation(hb, c):
        cs = slice(c * FFN_SUB, (c + 1) * FFN_SUB)
        gate = jnp.dot(hb, wg_ref[:, cs], preferred_element_type=F32)
        up = jnp.dot(hb, wu_ref[:, cs], preferred_element_type=F32)
        return (gate * (1.0 / (1.0 + jnp.exp(-gate))) * up).astype(BF16)

    @pl.when(f < last)
    def _():
        hb = h_ref[...].astype(BF16)
        part = None
        for c in range(tf // FFN_SUB):
            p = jnp.dot(activation(hb, c), wd_ref[c * FFN_SUB:(c + 1) * FFN_SUB, :], preferred_element_type=F32)
            part = p if part is None else part + p
        o_ref[...] += part

    @pl.when(f == last)
    def _():
        hb = h_ref[...].astype(BF16)
        a = jnp.concatenate([activation(hb, c) for c in range(tf // FFN_SUB)], axis=1)
        sub = min(tm, ROW_SUB)
        for r in range(tm // sub):
            rows = slice(r * sub, (r + 1) * sub)
            p = jnp.dot(a[rows], wd_ref[...], preferred_element_type=F32)
            o_ref[rows, :] = _layer_norm(o_ref[rows, :] + p, g_ref[...], b_ref[...])


def _ffn(h1, wg, wu, wd, ln_g, ln_b, tm, tf):
    rows = h1.shape[0]
    return pl.pallas_call(
        _ffn_kernel,
        grid=(rows // tm, D_FF // tf),
        in_specs=[
            pl.BlockSpec((tm, D_MODEL), lambda i, f: (i, 0)),
            pl.BlockSpec((D_MODEL, tf), lambda i, f: (0, f)),
            pl.BlockSpec((D_MODEL, tf), lambda i, f: (0, f)),
            pl.BlockSpec((tf, D_MODEL), lambda i, f: (f, 0)),
            pl.BlockSpec((1, D_MODEL), lambda i, f: (0, 0)),
            pl.BlockSpec((1, D_MODEL), lambda i, f: (0, 0)),
        ],
        out_specs=pl.BlockSpec((tm, D_MODEL), lambda i, f: (i, 0)),
        out_shape=jax.ShapeDtypeStruct((rows, D_MODEL), F32),
        compiler_params=_params(("parallel", "arbitrary")),
        name="ffn",
    )(h1, wg, wu, wd, ln_g, ln_b)


def _rope_tables(pos):
    half = ROT_DIM // 2
    inv_freq = ROPE_THETA ** (-np.arange(half, dtype=np.float64) * 2.0 / ROT_DIM)
    ang = np.asarray(pos, np.float64)[:, None] * inv_freq
    cos, sin = np.cos(ang), np.sin(ang)
    t = ang.shape[0]
    ones = np.ones((t, HEAD_DIM - ROT_DIM))
    zeros = np.zeros((t, HEAD_DIM - ROT_DIM))
    zh = np.zeros((t, half))
    c = np.concatenate([cos, cos, ones], -1)
    s_lo = np.concatenate([-sin, zh, zeros], -1)
    s_hi = np.concatenate([zh, sin, zeros], -1)
    return tuple(jnp.asarray(np.concatenate([a, a], -1), F32) for a in (c, s_lo, s_hi))


def kernel(x_prompt, x_sample, cache_k, cache_v, state_pool, meta_tokens, ln_in_g, ln_in_b, w_in, b_in,
           attn_sinks, w_pool, pool_scale, w_o, ln1_g, ln1_b, w_gate, w_up, w_down, ln2_g, ln2_b):
    bp, seq, _ = x_prompt.shape
    bs = x_sample.shape[0]
    wk = cache_k.shape[2]
    assert x_sample.shape[1] == 1 and cache_k.shape[0] == 1

    w_in_p = w_in[0].astype(BF16)
    b_in_p = b_in[0][None]
    wpool = w_pool[0].astype(BF16)
    pscale = pool_scale[0][None]
    sinks = attn_sinks[0]
    g_in, b_ln_in = ln_in_g[None], ln_in_b[None]
    g1, b1, g2, b2 = ln1_g[0][None], ln1_b[0][None], ln2_g[0][None], ln2_b[0][None]

    tabs_prompt = _rope_tables(N_META + np.arange(seq))
    tabs_small = _rope_tables(np.concatenate([np.arange(N_META), np.full((bs,), PAST_LEN)]))

    x_small = jnp.concatenate([meta_tokens, x_sample.reshape(bs, D_MODEL)], 0)
    h_ms, q_ms, k_ms, v_ms, u_ms = _inproj(x_small, g_in, b_ln_in, w_in_p, b_in_p, tabs_small, N_META + bs)
    k_meta, v_meta, u_meta = k_ms[:N_META], v_ms[:N_META], u_ms[:N_META]
    h_s, q_s, k_s, v_s, u_s = h_ms[N_META:], q_ms[N_META:], k_ms[N_META:], v_ms[N_META:], u_ms[N_META:]

    h_p, q_p, k_p, v_p, u_p, wg, wu, wd, w_o_p, k_tail, v_tail, u_tail = _inproj(
        x_prompt.reshape(bp * seq, D_MODEL), g_in, b_ln_in, w_in_p, b_in_p, tabs_prompt, TM_INPROJ,
        cast=(w_gate[0], w_up[0], w_down[0], w_o[0]), seq_steps=seq // TM_INPROJ)
    k_p3 = k_p.reshape(bp, seq, KV_COLS)
    v_p3 = v_p.reshape(bp, seq, KV_COLS)
    u_p3 = u_p.reshape(bp, seq, POOL_W)
    h1_p = _prompt_mixer(sinks, q_p.reshape(bp, seq, Q_COLS), k_p3, v_p3, u_p3, h_p,
                         k_meta, v_meta, u_meta, wpool, pscale, w_o_p, g1, b1, TM_MIXER)
    y_prompt = _ffn(h1_p, wg, wu, wd, g2, b2, TM_FFN, TF_FFN).reshape(bp, seq, D_MODEL)

    attn_s, nk_s, nv_s = _sample_attn(sinks, q_s, k_s, v_s, cache_k.reshape(bs, wk, KV_COLS),
                                      cache_v.reshape(bs, wk, KV_COLS), BB_SAMPLE)
    h1_s, npool_s = _sample_mix(attn_s, u_s, state_pool[0].swapaxes(0, 1), h_s, wpool, pscale, w_o_p, g1, b1)
    y_sample = _ffn(h1_s, wg, wu, wd, g2, b2, bs, TF_FFN).reshape(bs, 1, D_MODEL)

    new_k_prompt = k_tail.reshape(1, bp, WINDOW, N_KV, HEAD_DIM)
    new_v_prompt = v_tail.reshape(1, bp, WINDOW, N_KV, HEAD_DIM)
    new_pool_prompt = u_tail[:, N_META - POOL_HIST:][None]
    return (y_prompt, y_sample, new_k_prompt, new_v_prompt, new_pool_prompt,
            nk_s.reshape(1, bs, wk, N_KV, HEAD_DIM), nv_s.reshape(1, bs, wk, N_KV, HEAD_DIM), npool_s.swapaxes(0, 1)[None])
```
